```python
import jax, jax.numpy as jnp
from jax import lax
import numpy as np

D_MODEL = 1024
BATCH = 16
SEQ = 2048
DEPTH = 1
DEC_BATCH = 32
DEC_SEQ = 2048
PAST_LEN = 128

S5_WIDTH = D_MODEL // 2
S5_GROUP = 16
S5_GROUPS = S5_WIDTH // S5_GROUP
S5_STATE = 64
RWKV_WIDTH = D_MODEL // 2
RWKV_HEAD = 64
RWKV_HEADS = RWKV_WIDTH // RWKV_HEAD
DECAY_RANK = 64
ICLR_RANK = 64
GATE_RANK = 128
RWKV_IN_W = 3 * RWKV_WIDTH + 2 * DECAY_RANK + 2 * ICLR_RANK + GATE_RANK
RWKV_SPLITS = [RWKV_WIDTH, 2 * RWKV_WIDTH, 3 * RWKV_WIDTH,
               3 * RWKV_WIDTH + 2 * DECAY_RANK,
               3 * RWKV_WIDTH + 2 * DECAY_RANK + 2 * ICLR_RANK]
IN_COLS = S5_WIDTH + RWKV_IN_W + 2 * D_MODEL
D_FF = ((8 * D_MODEL + 3 * 256 - 1) // (3 * 256)) * 256
ALPHA = (2.0 * DEPTH) ** 0.25
BETA = (8.0 * DEPTH) ** -0.25
LN_EPS = 1e-5
GN_EPS = 64e-5
NORM_EPS = 1e-12

kernel_name = "hybrid_s5_rwkv7_deepnorm_encoder"


def _layer_norm(x, g, b):
    xf = x.astype(jnp.float32)
    mu = jnp.mean(xf, axis=-1, keepdims=True)
    var = jnp.mean(jnp.square(xf - mu), axis=-1, keepdims=True)
    return ((xf - mu) * lax.rsqrt(var + LN_EPS) * g.astype(jnp.float32) + b.astype(jnp.float32)).astype(x.dtype)


def _centred_shift(z, mu):
    prev = jnp.pad(z[:, :-1], ((0, 0), (1, 0), (0, 0)))
    nxt = jnp.pad(z[:, 1:], ((0, 0), (0, 1), (0, 0)))
    return z + (0.5 * (prev + nxt) - z) * mu


def _ssm_combine(left, right):
    a_l, b_l = left
    a_r, b_r = right
    return a_r * a_l, a_r * b_l + b_r


def _s5_branch(u, lam_re, lam_im, log_dt, b_re, b_im, c_re, c_im, d_skip, w_glu, b_glu):
    f32 = jnp.float32
    bsz, seqlen, _ = u.shape
    uf = u.astype(f32)
    ug = uf.reshape(bsz, seqlen, S5_GROUPS, S5_GROUP)
    y = d_skip.astype(f32) * uf
    for direction in range(2):
        lam = lax.complex(lam_re[direction].astype(f32), lam_im[direction].astype(f32))
        dt = jnp.exp(log_dt[direction].astype(f32))[:, None]
        lam_bar = jnp.exp(lam * dt)
        b_mat = lax.complex(b_re[direction].astype(f32), b_im[direction].astype(f32))
        b_bar = ((lam_bar - 1.0) / lam)[..., None] * b_mat
        bu = jnp.einsum('blgh,gph->blgp', ug, b_bar)
        a_elems = jnp.broadcast_to(lam_bar, (1, seqlen) + lam_bar.shape)
        _, states = lax.associative_scan(_ssm_combine, (a_elems, bu),
                                         reverse=bool(direction), axis=1)
        c_mat = lax.complex(c_re[direction].astype(f32), c_im[direction].astype(f32))
        y = y + jnp.einsum('blgp,ghp->blgh', states, c_mat).real.reshape(bsz, seqlen, S5_WIDTH)
    act = jax.nn.gelu(y)
    out = act * jax.nn.sigmoid(act @ w_glu.astype(f32) + b_glu.astype(f32))
    return out.astype(u.dtype)


def _heads(t):
    return t.reshape(t.shape[:-1] + (RWKV_HEADS, RWKV_HEAD))


def _rwkv7_step(state, inputs):
    r, w, k, v, a, b = inputs
    sa = jnp.einsum('bhij,bhj->bhi', state, a)
    state = state * w[:, :, None, :] + sa[..., None] * b[:, :, None, :] + v[..., None] * k[:, :, None, :]
    return state, jnp.einsum('bhij,bhj->bhi', state, r)


def _head_norm(y, g, b):
    mu = jnp.mean(y, axis=-1, keepdims=True)
    var = jnp.mean(jnp.square(y - mu), axis=-1, keepdims=True)
    return (y - mu) * lax.rsqrt(var + GN_EPS) * _heads(g.astype(jnp.float32)) + _heads(b.astype(jnp.float32))


def _rwkv7_branch(z, w0, w2, a0, a2, g2, k_k, k_a, r_k, gn_g, gn_b):
    f32 = jnp.float32
    z = z.astype(f32)
    bsz, seqlen = z.shape[0], z.shape[1]
    r, k, v, w_low, a_low, g_low = jnp.split(z, RWKV_SPLITS, axis=-1)
    g = jax.nn.sigmoid(g_low) @ g2.astype(f32)
    r_h, k_h, v_h = _heads(r), _heads(k), _heads(v)
    kk = _heads(k * k_k.astype(f32))
    kk = kk / jnp.maximum(jnp.linalg.norm(kk, axis=-1, keepdims=True), NORM_EPS)
    k_a_h = _heads(k_a.astype(f32))
    r_k = r_k.astype(f32)
    state0 = jnp.zeros((bsz, RWKV_HEADS, RWKV_HEAD, RWKV_HEAD), f32)
    y = jnp.zeros_like(r_h)
    bonus = jnp.zeros_like(r_h)
    for direction in range(2):
        w_d = w_low[..., direction * DECAY_RANK:(direction + 1) * DECAY_RANK]
        a_d = a_low[..., direction * ICLR_RANK:(direction + 1) * ICLR_RANK]
        w_log = w0[direction].astype(f32) + jnp.tanh(w_d) @ w2[direction].astype(f32)
        w_log = -jax.nn.softplus(-w_log) - 0.5
        decay = _heads(jnp.exp(-jnp.exp(w_log)))
        iclr = _heads(jax.nn.sigmoid(a0[direction].astype(f32) + a_d @ a2[direction].astype(f32)))
        k_dir = k_h * (1.0 + (iclr - 1.0) * k_a_h)
        xs = tuple(jnp.moveaxis(t, 1, 0) for t in (r_h, decay, k_dir, v_h, -kk, kk * iclr))
        _, ys = lax.scan(_rwkv7_step, state0, xs, reverse=bool(direction))
        y = y + jnp.moveaxis(ys, 0, 1)
        bonus = bonus + jnp.sum(r_h * k_dir * r_k, axis=-1, keepdims=True) * v_h
    y = _head_norm(y, gn_g, gn_b) + bonus
    return y.reshape(bsz, seqlen, RWKV_WIDTH) * g


def _encoder_layer(x, w_in, s5_lam_re, s5_lam_im, s5_log_dt, s5_b_re, s5_b_im, s5_c_re, s5_c_im,
                   s5_d, s5_w_glu, s5_b_glu, rwkv_mu, rwkv_w0, rwkv_w2, rwkv_a0, rwkv_a2, rwkv_g2,
                   rwkv_k_k, rwkv_k_a, rwkv_r_k, rwkv_gn_g, rwkv_gn_b, w_pa, w_pb, w_out,
                   ln1_g, ln1_b, w_gate, w_up, w_down, ln2_g, ln2_b):
    proj = x @ w_in
    u_s5, z_rwkv, gate_a, gate_b = jnp.split(
        proj, [S5_WIDTH, S5_WIDTH + RWKV_IN_W, S5_WIDTH + RWKV_IN_W + D_MODEL], axis=-1)
    out_a = _s5_branch(u_s5, s5_lam_re, s5_lam_im, s5_log_dt, s5_b_re, s5_b_im,
                       s5_c_re, s5_c_im, s5_d, s5_w_glu, s5_b_glu)
    out_b = _rwkv7_branch(_centred_shift(z_rwkv, rwkv_mu), rwkv_w0, rwkv_w2, rwkv_a0, rwkv_a2,
                          rwkv_g2, rwkv_k_k, rwkv_k_a, rwkv_r_k, rwkv_gn_g, rwkv_gn_b).astype(x.dtype)
    merged = jax.nn.sigmoid(gate_a) * (out_a @ w_pa) + jax.nn.sigmoid(gate_b) * (out_b @ w_pb)
    h = _layer_norm(ALPHA * x + merged @ w_out, ln1_g, ln1_b)
    ffn = (jax.nn.silu(h @ w_gate) * (h @ w_up)) @ w_down
    return _layer_norm(ALPHA * h + ffn, ln2_g, ln2_b)


def _normal(k, shape, scale):
    return scale * jax.random.normal(k, shape, jnp.float32)


def setup_inputs(seed: int = 0) -> dict:
    key = jax.random.key(seed)
    ks = iter(jax.random.split(key, 40))
    L, G, P, H = DEPTH, S5_GROUPS, S5_STATE, S5_GROUP
    lam_im_base = jnp.pi * jnp.arange(P, dtype=jnp.float32)
    return {
        "x_prompt": jax.random.normal(next(ks), (BATCH, SEQ, D_MODEL), jnp.float32),
        "x_sample": jax.random.normal(next(ks), (DEC_BATCH, DEC_SEQ, D_MODEL), jnp.float32),
        "w_in": _normal(next(ks), (L, D_MODEL, IN_COLS), D_MODEL ** -0.5),
        "s5_lam_re": -0.5 + _normal(next(ks), (L, 2, G, P), 0.01),
        "s5_lam_im": lam_im_base + _normal(next(ks), (L, 2, G, P), 0.01),
        "s5_log_dt": jax.random.uniform(next(ks), (L, 2, G), jnp.float32,
                                        minval=float(np.log(1e-3)), maxval=float(np.log(1e-1))),
        "s5_b_re": _normal(next(ks), (L, 2, G, P, H), (2.0 * H) ** -0.5),
        "s5_b_im": _normal(next(ks), (L, 2, G, P, H), (2.0 * H) ** -0.5),
        "s5_c_re": _normal(next(ks), (L, 2, G, H, P), (2.0 * P) ** -0.5),
        "s5_c_im": _normal(next(ks), (L, 2, G, H, P), (2.0 * P) ** -0.5),
        "s5_d": _normal(next(ks), (L, S5_WIDTH), 1.0),
        "s5_w_glu": _normal(next(ks), (L, S5_WIDTH, S5_WIDTH), S5_WIDTH ** -0.5),
        "s5_b_glu": _normal(next(ks), (L, S5_WIDTH), 0.01),
        "rwkv_mu": jax.random.uniform(next(ks), (L, RWKV_IN_W), jnp.float32),
        "rwkv_w0": jax.random.uniform(next(ks), (L, 2, RWKV_WIDTH), jnp.float32, minval=-6.0, maxval=-1.0),
        "rwkv_w2": _normal(next(ks), (L, 2, DECAY_RANK, RWKV_WIDTH), 0.5 * DECAY_RANK ** -0.5),
        "rwkv_a0": _normal(next(ks), (L, 2, RWKV_WIDTH), 0.1),
        "rwkv_a2": _normal(next(ks), (L, 2, ICLR_RANK, RWKV_WIDTH), 0.5 * ICLR_RANK ** -0.5),
        "rwkv_g2": _normal(next(ks), (L, GATE_RANK, RWKV_WIDTH), GATE_RANK ** -0.5),
        "rwkv_k_k": 0.85 + _normal(next(ks), (L, RWKV_WIDTH), 0.02),
        "rwkv_k_a": 1.0 + _normal(next(ks), (L, RWKV_WIDTH), 0.02),
        "rwkv_r_k": _normal(next(ks), (L, RWKV_HEADS, RWKV_HEAD), 0.1),
        "rwkv_gn_g": 1.0 + _normal(next(ks), (L, RWKV_WIDTH), 0.02),
        "rwkv_gn_b": _normal(next(ks), (L, RWKV_WIDTH), 0.01),
        "w_pa": _normal(next(ks), (L, S5_WIDTH, D_MODEL), S5_WIDTH ** -0.5),
        "w_pb": _normal(next(ks), (L, RWKV_WIDTH, D_MODEL), RWKV_WIDTH ** -0.5),
        "w_out": _normal(next(ks), (L, D_MODEL, D_MODEL), BETA * D_MODEL ** -0.5),
        "ln1_g": 1.0 + _normal(next(ks), (L, D_MODEL), 0.02),
        "ln1_b": _normal(next(ks), (L, D_MODEL), 0.01),
        "w_gate": _normal(next(ks), (L, D_MODEL, D_FF), D_MODEL ** -0.5),
        "w_up": _normal(next(ks), (L, D_MODEL, D_FF), D_MODEL ** -0.5),
        "w_down": _normal(next(ks), (L, D_FF, D_MODEL), BETA * D_FF ** -0.5),
        "ln2_g": 1.0 + _normal(next(ks), (L, D_MODEL), 0.02),
        "ln2_b": _normal(next(ks), (L, D_MODEL), 0.01),
    }


def reference(x_prompt, x_sample, w_in, s5_lam_re, s5_lam_im, s5_log_dt, s5_b_re, s5_b_im,
              s5_c_re, s5_c_im, s5_d, s5_w_glu, s5_b_glu, rwkv_mu, rwkv_w0, rwkv_w2, rwkv_a0,
              rwkv_a2, rwkv_g2, rwkv_k_k, rwkv_k_a, rwkv_r_k, rwkv_gn_g, rwkv_gn_b, w_pa, w_pb,
              w_out, ln1_g, ln1_b, w_gate, w_up, w_down, ln2_g, ln2_b):
    weights = (w_in, s5_lam_re, s5_lam_im, s5_log_dt, s5_b_re, s5_b_im, s5_c_re, s5_c_im,
               s5_d, s5_w_glu, s5_b_glu, rwkv_mu, rwkv_w0, rwkv_w2, rwkv_a0, rwkv_a2, rwkv_g2,
               rwkv_k_k, rwkv_k_a, rwkv_r_k, rwkv_gn_g, rwkv_gn_b, w_pa, w_pb, w_out,
               ln1_g, ln1_b, w_gate, w_up, w_down, ln2_g, ln2_b)
    y_prompt = x_prompt
    y_sample = x_sample
    for layer in range(DEPTH):
        layer_weights = [w[layer] for w in weights]
        y_prompt = _encoder_layer(y_prompt, *layer_weights)
        y_sample = _encoder_layer(y_sample, *layer_weights)
    return (y_prompt, y_sample)
```

```python
import functools

import jax
import jax.numpy as jnp
from jax import lax
from jax.experimental import pallas as pl
from jax.experimental.pallas import tpu as pltpu

F32 = jnp.float32
MXU_DTYPE = jnp.bfloat16

D_MODEL = 1024
S5_WIDTH = 512
S5_GROUP = 16
S5_GROUPS = S5_WIDTH // S5_GROUP
S5_STATE = 64
RWKV_WIDTH = 512
RWKV_HEAD = 64
DECAY_RANK = 64
ICLR_RANK = 64
GATE_RANK = 128
RWKV_IN_W = 3 * RWKV_WIDTH + 2 * DECAY_RANK + 2 * ICLR_RANK + GATE_RANK
D_FF = 2816
DEPTH = 1
ALPHA = (2.0 * DEPTH) ** 0.25
LN_EPS = 1e-5
GN_EPS = 64e-5
NORM_EPS = 1e-12

S5_CHUNK = 16
S5_CW = S5_CHUNK * S5_GROUP
RWKV_CHUNK = 64
PAIR = 2 * RWKV_HEAD
N_PAIRS = RWKV_WIDTH // PAIR
TOKEN_TILE = 512
VMEM_LIMIT = 56 * 1024 * 1024


def _mx(a):
    return a.astype(MXU_DTYPE)


def _dot(a, b):
    return jnp.dot(_mx(a), _mx(b), preferred_element_type=F32)


def _dot_nt(a, b):
    return lax.dot_general(_mx(a), _mx(b), (((1,), (1,)), ((), ())), preferred_element_type=F32)


def _split3(x):
    hi = x.astype(MXU_DTYPE)
    r1 = x - hi.astype(F32)
    mid = r1.astype(MXU_DTYPE)
    lo = (r1 - mid.astype(F32)).astype(MXU_DTYPE)
    return hi, mid, lo


def _dot_exact_rhs(x, m):
    hi, mid, lo = _split3(x)
    return (jnp.dot(hi, m, preferred_element_type=F32) + jnp.dot(mid, m, preferred_element_type=F32)
            + jnp.dot(lo, m, preferred_element_type=F32))


def _dot_exact_lhs(m, x):
    hi, mid, lo = _split3(x)
    return (jnp.dot(m, hi, preferred_element_type=F32) + jnp.dot(m, mid, preferred_element_type=F32)
            + jnp.dot(m, lo, preferred_element_type=F32))


def _layer_norm(v, g, b):
    mu = jnp.mean(v, axis=-1, keepdims=True)
    d = v - mu
    var = jnp.mean(d * d, axis=-1, keepdims=True)
    return d * lax.rsqrt(var + LN_EPS) * g + b


def _gelu_tanh(v):
    return 0.5 * v * (1.0 + jnp.tanh(0.7978845608028654 * (v + 0.044715 * (v * v * v))))


def _const_spec(shape):
    nd = len(shape)
    return pl.BlockSpec(shape, lambda *_: (0,) * nd)


def _proj_kernel(x_ref, wu_ref, wz_ref, u_ref, z_ref):
    xb = _mx(x_ref[...])
    u_ref[...] = jnp.dot(xb, wu_ref[...], preferred_element_type=F32)
    z_ref[...] = jnp.dot(xb, wz_ref[...], preferred_element_type=F32)


def _proj(x2, wu, wz):
    n = x2.shape[0]
    tm = min(TOKEN_TILE, n)
    return pl.pallas_call(
        _proj_kernel,
        grid=(n // tm,),
        in_specs=[pl.BlockSpec((tm, D_MODEL), lambda i: (i, 0)),
                  _const_spec(wu.shape), _const_spec(wz.shape)],
        out_specs=[pl.BlockSpec((tm, S5_WIDTH), lambda i: (i, 0)),
                   pl.BlockSpec((tm, RWKV_IN_W), lambda i: (i, 0))],
        out_shape=[jax.ShapeDtypeStruct((n, S5_WIDTH), F32),
                   jax.ShapeDtypeStruct((n, RWKV_IN_W), F32)],
        compiler_params=pltpu.CompilerParams(dimension_semantics=("arbitrary",),
                                             vmem_limit_bytes=VMEM_LIMIT),
    )(x2, wu, wz)


def _s5_chunk_matrices(lam_re, lam_im, log_dt, b_re, b_im, c_re, c_im, d_skip):
    hp = lax.Precision.HIGHEST
    t = S5_CHUNK
    g = S5_GROUPS
    dt = jnp.exp(log_dt.astype(F32))[..., None]
    lre = lam_re.astype(F32)
    lim = lam_im.astype(F32)
    zr = lre * dt
    zi = lim * dt
    n = jnp.arange(t + 1, dtype=F32)[:, None, None, None]
    mag = jnp.exp(n * zr)
    pr = mag * jnp.cos(n * zi)
    pi = mag * jnp.sin(n * zi)
    den = lre * lre + lim * lim
    lbr = pr[1] - 1.0
    lbi = pi[1]
    qr = (lbr * lre + lbi * lim) / den
    qi = (lbi * lre - lbr * lim) / den
    bre = b_re.astype(F32)
    bim = b_im.astype(F32)
    bbr = qr[..., None] * bre - qi[..., None] * bim
    bbi = qr[..., None] * bim + qi[..., None] * bre
    wr = pr[..., None] * bbr - pi[..., None] * bbi
    wi = pr[..., None] * bbi + pi[..., None] * bbr
    cre = c_re.astype(F32)
    cim = c_im.astype(F32)
    kern = (jnp.einsum('dghp,ndgpk->ndghk', cre, wr[:t], precision=hp)
            - jnp.einsum('dghp,ndgpk->ndghk', cim, wi[:t], precision=hp))
    jj = jnp.arange(t)[:, None]
    ii = jnp.arange(t)[None, :]
    kf = kern[:, 0][jnp.clip(ii - jj, 0, t - 1)]
    kb = kern[:, 1][jnp.clip(jj - ii, 0, t - 1)]
    kf = jnp.where((ii >= jj)[:, :, None, None, None], kf, 0.0)
    kb = jnp.where((jj >= ii)[:, :, None, None, None], kb, 0.0)
    t_mat = jnp.transpose(kf + kb, (2, 0, 4, 1, 3)).reshape(g, S5_CW, S5_CW)

    def rows_jh(a):
        return jnp.transpose(a, (1, 0, 3, 2)).reshape(g, S5_CW, S5_STATE)

    m_in = jnp.concatenate([rows_jh(wr[:t, 0][::-1]), rows_jh(wr[:t, 1]),
                            rows_jh(wi[:t, 0][::-1]), rows_jh(wi[:t, 1])], axis=-1)

    def cols_ih(a):
        return jnp.transpose(a, (1, 3, 0, 2)).reshape(g, S5_STATE, S5_CW)

    pf_r, pf_i = pr[1:, 0][:, :, None, :], pi[1:, 0][:, :, None, :]
    pb_r, pb_i = pr[1:, 1][::-1][:, :, None, :], pi[1:, 1][::-1][:, :, None, :]
    m_out = jnp.concatenate([
        cols_ih(cre[0][None] * pf_r - cim[0][None] * pf_i),
        cols_ih(cre[1][None] * pb_r - cim[1][None] * pb_i),
        cols_ih(-(cre[0][None] * pf_i + cim[0][None] * pf_r)),
        cols_ih(-(cre[1][None] * pb_i + cim[1][None] * pb_r))], axis=1)
    a_step = jnp.stack([jnp.concatenate([pr[t, 0], pr[t, 1]], axis=-1),
                        jnp.concatenate([pi[t, 0], pi[t, 1]], axis=-1)], axis=1)
    d_t = jnp.tile(d_skip.astype(F32).reshape(g, 1, S5_GROUP), (1, t, 1)).reshape(g, 1, S5_CW)
    return _mx(t_mat), _mx(m_in), _mx(m_out), a_step, d_t


def _s5_kernel(u_ref, t_ref, min_ref, mout_ref, a_ref, d_ref, y_ref, sin_ref, xs_ref, *, bsz, n_chunks):
    u = u_ref[0]
    ub = _mx(u)
    sin_ref[...] = jnp.dot(ub, min_ref[0], preferred_element_type=F32)
    a_re = a_ref[0, 0:1, :]
    a_im = a_ref[0, 1:2, :]
    half = S5_STATE
    fwd_lane = lax.broadcasted_iota(jnp.int32, (bsz, 2 * half), 1) < half

    def step(k, carry):
        c_re, c_im = carry
        rk = pl.ds(pl.multiple_of(k * bsz, bsz), bsz)
        rr = pl.ds(pl.multiple_of((n_chunks - 1 - k) * bsz, bsz), bsz)
        xs_ref[rk, 0:half] = c_re[:, 0:half]
        xs_ref[rr, half:2 * half] = c_re[:, half:2 * half]
        xs_ref[rk, 2 * half:3 * half] = c_im[:, 0:half]
        xs_ref[rr, 3 * half:4 * half] = c_im[:, half:2 * half]
        s_re = jnp.where(fwd_lane, sin_ref[rk, 0:2 * half], sin_ref[rr, 0:2 * half])
        s_im = jnp.where(fwd_lane, sin_ref[rk, 2 * half:4 * half], sin_ref[rr, 2 * half:4 * half])
        n_re = a_re * c_re - a_im * c_im + s_re
        n_im = a_re * c_im + a_im * c_re + s_im
        return n_re, n_im

    zero = jnp.zeros((bsz, 2 * half), F32)
    lax.fori_loop(0, n_chunks, step, (zero, zero))
    y = jnp.dot(ub, t_ref[0], preferred_element_type=F32)
    y = y + jnp.dot(_mx(xs_ref[...]), mout_ref[0], preferred_element_type=F32)
    y_ref[0] = y + u * d_ref[0]


def _s5_scan(u_t, mats, bsz, n_chunks):
    t_mat, m_in, m_out, a_step, d_t = mats
    g, rows, cw = u_t.shape
    mat_spec = pl.BlockSpec((1, cw, cw), lambda i: (i, 0, 0))
    return pl.pallas_call(
        functools.partial(_s5_kernel, bsz=bsz, n_chunks=n_chunks),
        grid=(g,),
        in_specs=[pl.BlockSpec((1, rows, cw), lambda i: (i, 0, 0)), mat_spec, mat_spec, mat_spec,
                  pl.BlockSpec((1, 2, 2 * S5_STATE), lambda i: (i, 0, 0)),
                  pl.BlockSpec((1, 1, cw), lambda i: (i, 0, 0))],
        out_specs=pl.BlockSpec((1, rows, cw), lambda i: (i, 0, 0)),
        out_shape=jax.ShapeDtypeStruct((g, rows, cw), F32),
        scratch_shapes=[pltpu.VMEM((rows, cw), F32), pltpu.VMEM((rows, cw), F32)],
        compiler_params=pltpu.CompilerParams(dimension_semantics=("arbitrary",),
                                             vmem_limit_bytes=VMEM_LIMIT),
    )(u_t, t_mat, m_in, m_out, a_step, d_t)


def _s5_branch(u, mats):
    bsz, seqlen, _ = u.shape
    n_chunks = seqlen // S5_CHUNK
    u_t = u.reshape(bsz, n_chunks, S5_CHUNK, S5_GROUPS, S5_GROUP)
    u_t = jnp.transpose(u_t, (3, 1, 0, 2, 4)).reshape(S5_GROUPS, n_chunks * bsz, S5_CW)
    y_t = _s5_scan(u_t, mats, bsz, n_chunks)
    y = y_t.reshape(S5_GROUPS, n_chunks, bsz, S5_CHUNK, S5_GROUP)
    return jnp.transpose(y, (2, 1, 3, 0, 4)).reshape(bsz, seqlen, S5_WIDTH)


def _rwkv_kernel(z_ref, mu_ref, w0_ref, w2_ref, a0_ref, a2_ref, g2_ref, kk_ref, ka_ref, rk_ref,
                 gng_ref, gnb_ref, o_ref, s_ref, y_ref, bon_ref, *, seqlen):
    t = RWKV_CHUNK
    n_chunks = seqlen // t
    w = RWKV_WIDTH

    row = lax.broadcasted_iota(jnp.int32, (t, 1), 0)
    tri_r = lax.broadcasted_iota(jnp.int32, (t, t), 0)
    tri_c = lax.broadcasted_iota(jnp.int32, (t, t), 1)
    seg_r = lax.broadcasted_iota(jnp.int32, (w, w), 0) // RWKV_HEAD
    seg_c = lax.broadcasted_iota(jnp.int32, (w, w), 1) // RWKV_HEAD
    seg_ones = jnp.where(seg_r == seg_c, 1.0, 0.0).astype(MXU_DTYPE)
    pr = lax.broadcasted_iota(jnp.int32, (PAIR, PAIR), 0)
    pc = lax.broadcasted_iota(jnp.int32, (PAIR, PAIR), 1)
    same_head = (pr // t) == (pc // t)
    eye = jnp.where(pr == pc, 1.0, 0.0).astype(F32)
    first_head = lax.broadcasted_iota(jnp.int32, (t, PAIR), 1) < RWKV_HEAD
    cum_ops = (jnp.where(tri_r >= tri_c, 1.0, 0.0).astype(MXU_DTYPE),
               jnp.where(tri_r <= tri_c, 1.0, 0.0).astype(MXU_DTYPE))
    strict = (same_head & (pr > pc), same_head & (pr < pc))
    incl = (same_head & (pr >= pc), same_head & (pr <= pc))

    mu = mu_ref[...]
    k_k = kk_ref[...]
    k_a = ka_ref[...]
    r_k = rk_ref[...]

    s_ref[...] = jnp.zeros_like(s_ref)
    y_ref[...] = jnp.zeros_like(y_ref)
    bon_ref[...] = jnp.zeros_like(bon_ref)

    def stack(a, p):
        ap = a[:, p * PAIR:(p + 1) * PAIR]
        return jnp.concatenate([jnp.where(first_head, ap, 0.0), jnp.where(first_head, 0.0, ap)], axis=0)

    def chunk_step(ci, d):
        base = pl.multiple_of(ci * t, t)
        rows = pl.ds(base, t)
        zc = z_ref[0, rows, :]
        prev_blk = z_ref[0, pl.ds(pl.multiple_of(jnp.maximum(base - 8, 0), 8), 8), :]
        next_blk = z_ref[0, pl.ds(pl.multiple_of(jnp.minimum(base + t, seqlen - 8), 8), 8), :]
        prev_row = prev_blk[7:8, :] * (ci > 0).astype(F32)
        next_row = next_blk[0:1, :] * (ci < n_chunks - 1).astype(F32)
        prev = jnp.where(row == 0, prev_row, pltpu.roll(zc, 1, 0))
        nxt = jnp.where(row == t - 1, next_row, pltpu.roll(zc, t - 1, 0))
        zs = zc + (0.5 * (prev + nxt) - zc) * mu
        r = zs[:, 0:w]
        k = zs[:, w:2 * w]
        v = zs[:, 2 * w:3 * w]
        w_low = zs[:, 3 * w:3 * w + 2 * DECAY_RANK]
        a_low = zs[:, 3 * w + 2 * DECAY_RANK:3 * w + 2 * DECAY_RANK + 2 * ICLR_RANK]

        w_log = w0_ref[d:d + 1, :] + _dot(jnp.tanh(w_low), w2_ref[d])
        w_log = -jnp.logaddexp(-w_log, 0.0) - 0.5
        logw = -jnp.exp(w_log)
        iclr = jax.nn.sigmoid(a0_ref[d:d + 1, :] + _dot(a_low, a2_ref[d]))
        kk = k * k_k
        nrm = jnp.sqrt(_dot_exact_rhs(kk * kk, seg_ones))
        kk = kk / jnp.maximum(nrm, NORM_EPS)
        k_dir = k * (1.0 + (iclr - 1.0) * k_a)
        bon_ref[rows, :] += _dot_exact_rhs(r * k_dir * r_k, seg_ones) * v

        cum = _dot_exact_lhs(cum_ops[d], logw)
        last = cum[t - 1:t, :] if d == 0 else cum[0:1, :]
        e_neg = jnp.exp(-cum)
        a_t = -kk * jnp.exp(cum - logw)
        r_t = r * jnp.exp(cum)
        b_dir = kk * iclr
        b_t = b_dir * e_neg
        k_t = k_dir * e_neg
        e_end = jnp.exp(last - cum)
        b_e = b_dir * e_end
        k_e = k_dir * e_end
        g_end = jnp.exp(last)

        for p in range(N_PAIRS):
            a_s, r_s, b_s, k_s, v_s = (_mx(stack(q, p)) for q in (a_t, r_t, b_t, k_t, v))
            be_s, ke_s = _mx(stack(b_e, p)), _mx(stack(k_e, p))
            sc = _dot_nt(jnp.concatenate([a_s, r_s], axis=0), jnp.concatenate([b_s, k_s], axis=0))
            l_ab = jnp.where(strict[d], sc[0:PAIR, 0:PAIR], 0.0)
            l_ak = jnp.where(strict[d], sc[0:PAIR, PAIR:2 * PAIR], 0.0)
            m_rb = jnp.where(incl[d], sc[PAIR:2 * PAIR, 0:PAIR], 0.0)
            m_rk = jnp.where(incl[d], sc[PAIR:2 * PAIR, PAIR:2 * PAIR], 0.0)
            inv = eye + l_ab
            lp = l_ab
            for _ in range(5):
                lp = _dot(lp, lp)
                inv = inv + _dot(lp, inv)
            s_old = s_ref[d, p]
            s_mx = _mx(s_old)
            u_s = _dot(inv, _dot_nt(a_s, s_mx) + _dot(l_ak, v_s))
            o_s = _dot_nt(r_s, s_mx) + _dot(m_rb, u_s) + _dot(m_rk, v_s)
            y_ref[rows, p * PAIR:(p + 1) * PAIR] += o_s[0:t] + o_s[t:2 * t]
            s_ref[d, p] = (s_old * g_end[:, p * PAIR:(p + 1) * PAIR]
                           + _dot(u_s.T, be_s) + _dot(v_s.astype(F32).T, ke_s))
        return zs

    def body(ci, carry):
        zs = chunk_step(ci, 0)
        o_ref[0, pl.ds(pl.multiple_of(ci * t, t), t), :] = _dot(
            jax.nn.sigmoid(zs[:, 3 * w + 2 * DECAY_RANK + 2 * ICLR_RANK:]), g2_ref[...])
        chunk_step(n_chunks - 1 - ci, 1)
        return carry

    lax.fori_loop(0, n_chunks, body, 0)

    blk = min(seqlen, 256)
    gn_g = gng_ref[...]
    gn_b = gnb_ref[...]
    seg_mean = (seg_ones.astype(F32) * (1.0 / RWKV_HEAD)).astype(MXU_DTYPE)

    def norm_body(i, carry):
        rows = pl.ds(pl.multiple_of(i * blk, blk), blk)
        y = y_ref[rows, :]
        dlt = y - _dot_exact_rhs(y, seg_mean)
        var = _dot_exact_rhs(dlt * dlt, seg_mean)
        yn = dlt * lax.rsqrt(var + GN_EPS) * gn_g + gn_b
        o_ref[0, rows, :] = (yn + bon_ref[rows, :]) * o_ref[0, rows, :]
        return carry

    lax.fori_loop(0, seqlen // blk, norm_body, 0)


def _rwkv_branch(z, p):
    bsz, seqlen, _ = z.shape
    consts = [p["mu"], p["w0"], p["w2"], p["a0"], p["a2"], p["g2"], p["k_k"], p["k_a"], p["r_k"],
              p["gn_g"], p["gn_b"]]
    return pl.pallas_call(
        functools.partial(_rwkv_kernel, seqlen=seqlen),
        grid=(bsz,),
        in_specs=[pl.BlockSpec((1, seqlen, RWKV_IN_W), lambda i: (i, 0, 0))]
                 + [_const_spec(c.shape) for c in consts],
        out_specs=pl.BlockSpec((1, seqlen, RWKV_WIDTH), lambda i: (i, 0, 0)),
        out_shape=jax.ShapeDtypeStruct((bsz, seqlen, RWKV_WIDTH), F32),
        scratch_shapes=[pltpu.VMEM((2, N_PAIRS, PAIR, PAIR), F32),
                        pltpu.VMEM((seqlen, RWKV_WIDTH), F32),
                        pltpu.VMEM((seqlen, RWKV_WIDTH), F32)],
        compiler_params=pltpu.CompilerParams(dimension_semantics=("arbitrary",),
                                             vmem_limit_bytes=VMEM_LIMIT),
    )(z, *consts)


def _merge_kernel(x_ref, ys_ref, yr_ref, wg_ref, wglu_ref, bglu_ref, wpa_ref, wpb_ref, wout_ref,
                  g_ref, b_ref, h_ref):
    x = x_ref[...]
    xb = _mx(x)
    act = _gelu_tanh(ys_ref[...])
    out_a = act * jax.nn.sigmoid(_dot(act, wglu_ref[...]) + bglu_ref[...])
    gate_a = jax.nn.sigmoid(jnp.dot(xb, wg_ref[:, 0:D_MODEL], preferred_element_type=F32))
    merged = gate_a * _dot(out_a, wpa_ref[...])
    gate_b = jax.nn.sigmoid(jnp.dot(xb, wg_ref[:, D_MODEL:2 * D_MODEL], preferred_element_type=F32))
    merged = merged + gate_b * _dot(yr_ref[...], wpb_ref[...])
    h_ref[...] = _layer_norm(ALPHA * x + _dot(merged, wout_ref[...]), g_ref[...], b_ref[...])


def _merge(x2, ys, yr, p):
    n = x2.shape[0]
    tm = min(TOKEN_TILE, n)
    consts = [p["w_gates"], p["w_glu"], p["b_glu"], p["w_pa"], p["w_pb"], p["w_out"], p["ln1_g"], p["ln1_b"]]
    return pl.pallas_call(
        _merge_kernel,
        grid=(n // tm,),
        in_specs=[pl.BlockSpec((tm, D_MODEL), lambda i: (i, 0)),
                  pl.BlockSpec((tm, S5_WIDTH), lambda i: (i, 0)),
                  pl.BlockSpec((tm, RWKV_WIDTH), lambda i: (i, 0))]
                 + [_const_spec(c.shape) for c in consts],
        out_specs=pl.BlockSpec((tm, D_MODEL), lambda i: (i, 0)),
        out_shape=jax.ShapeDtypeStruct((n, D_MODEL), F32),
        compiler_params=pltpu.CompilerParams(dimension_semantics=("arbitrary",),
                                             vmem_limit_bytes=VMEM_LIMIT),
    )(x2, ys, yr, *consts)


def _ffn_kernel(h_ref, wg_ref, wu_ref, wd_ref, g_ref, b_ref, y_ref):
    h = h_ref[...]
    hb = _mx(h)
    half = D_FF // 2
    acc = ALPHA * h
    for c in range(2):
        cols = slice(c * half, (c + 1) * half)
        gate = jnp.dot(hb, wg_ref[:, cols], preferred_element_type=F32)
        up = jnp.dot(hb, wu_ref[:, cols], preferred_element_type=F32)
        acc = acc + _dot(gate * jax.nn.sigmoid(gate) * up, wd_ref[cols, :])
    y_ref[...] = _layer_norm(acc, g_ref[...], b_ref[...])


def _ffn(h2, p):
    n = h2.shape[0]
    tm = min(TOKEN_TILE, n)
    consts = [p["w_gate"], p["w_up"], p["w_down"], p["ln2_g"], p["ln2_b"]]
    return pl.pallas_call(
        _ffn_kernel,
        grid=(n // tm,),
        in_specs=[pl.BlockSpec((tm, D_MODEL), lambda i: (i, 0))] + [_const_spec(c.shape) for c in consts],
        out_specs=pl.BlockSpec((tm, D_MODEL), lambda i: (i, 0)),
        out_shape=jax.ShapeDtypeStruct((n, D_MODEL), F32),
        compiler_params=pltpu.CompilerParams(dimension_semantics=("arbitrary",),
                                             vmem_limit_bytes=VMEM_LIMIT),
    )(h2, *consts)


def _pad_rank(w2):
    z = jnp.zeros_like(w2[0])
    return jnp.stack([jnp.concatenate([w2[0], z], axis=0), jnp.concatenate([z, w2[1]], axis=0)])


def _row(v):
    return v.astype(F32).reshape(1, -1)


def _encoder_layer(x, p):
    bsz, seqlen, _ = x.shape
    x2 = x.reshape(bsz * seqlen, D_MODEL)
    u, z = _proj(x2, p["w_u"], p["w_z"])
    ys = _s5_branch(u.reshape(bsz, seqlen, S5_WIDTH), p["s5_mats"]).reshape(bsz * seqlen, S5_WIDTH)
    yr = _rwkv_branch(z.reshape(bsz, seqlen, RWKV_IN_W), p).reshape(bsz * seqlen, RWKV_WIDTH)
    h = _merge(x2, ys, yr, p)
    return _ffn(h, p).reshape(bsz, seqlen, D_MODEL)


def kernel(x_prompt, x_sample, w_in, s5_lam_re, s5_lam_im, s5_log_dt, s5_b_re, s5_b_im, s5_c_re, s5_c_im,
           s5_d, s5_w_glu, s5_b_glu, rwkv_mu, rwkv_w0, rwkv_w2, rwkv_a0, rwkv_a2, rwkv_g2, rwkv_k_k,
           rwkv_k_a, rwkv_r_k, rwkv_gn_g, rwkv_gn_b, w_pa, w_pb, w_out, ln1_g, ln1_b, w_gate, w_up,
           w_down, ln2_g, ln2_b):
    y_prompt, y_sample = x_prompt, x_sample
    for layer in range(w_in.shape[0]):
        wl = w_in[layer]
        gate_start = S5_WIDTH + RWKV_IN_W
        p = {
            "w_u": _mx(wl[:, :S5_WIDTH]),
            "w_z": _mx(wl[:, S5_WIDTH:gate_start]),
            "w_gates": _mx(wl[:, gate_start:]),
            "s5_mats": _s5_chunk_matrices(s5_lam_re[layer], s5_lam_im[layer], s5_log_dt[layer],
                                          s5_b_re[layer], s5_b_im[layer], s5_c_re[layer], s5_c_im[layer],
                                          s5_d[layer]),
            "w_glu": _mx(s5_w_glu[layer]), "b_glu": _row(s5_b_glu[layer]),
            "mu": _row(rwkv_mu[layer]), "w0": rwkv_w0[layer].astype(F32),
            "w2": _mx(_pad_rank(rwkv_w2[layer])), "a0": rwkv_a0[layer].astype(F32),
            "a2": _mx(_pad_rank(rwkv_a2[layer])), "g2": _mx(rwkv_g2[layer]),
            "k_k": _row(rwkv_k_k[layer]), "k_a": _row(rwkv_k_a[layer]), "r_k": _row(rwkv_r_k[layer]),
            "gn_g": _row(rwkv_gn_g[layer]), "gn_b": _row(rwkv_gn_b[layer]),
            "w_pa": _mx(w_pa[layer]), "w_pb": _mx(w_pb[layer]), "w_out": _mx(w_out[layer]),
            "ln1_g": _row(ln1_g[layer]), "ln1_b": _row(ln1_b[layer]),
            "w_gate": _mx(w_gate[layer]), "w_up": _mx(w_up[layer]), "w_down": _mx(w_down[layer]),
            "ln2_g": _row(ln2_g[layer]), "ln2_b": _row(ln2_b[layer]),
        }
        y_prompt = _encoder_layer(y_prompt, p)
        y_sample = _encoder_layer(y_sample, p)
    return (y_prompt, y_sample)
```

```python
import functools

import jax
import jax.numpy as jnp
from jax import lax
from jax.experimental import pallas as pl
from jax.experimental.pallas import tpu as pltpu

F32 = jnp.float32
MXU_DTYPE = jnp.bfloat16

D_MODEL = 1024
S5_WIDTH = 512
S5_GROUP = 16
S5_GROUPS = S5_WIDTH // S5_GROUP
S5_STATE = 64
RWKV_WIDTH = 512
RWKV_HEAD = 64
DECAY_RANK = 64
ICLR_RANK = 64
GATE_RANK = 128
RWKV_IN_W = 3 * RWKV_WIDTH + 2 * DECAY_RANK + 2 * ICLR_RANK + GATE_RANK
D_FF = 2816
DEPTH = 1
ALPHA = (2.0 * DEPTH) ** 0.25
LN_EPS = 1e-5
GN_EPS = 64e-5
NORM_EPS = 1e-12

S5_CHUNK = 16
S5_CW = S5_CHUNK * S5_GROUP
RWKV_CHUNK = 64
PAIR = 2 * RWKV_HEAD
N_PAIRS = RWKV_WIDTH // PAIR
TOKEN_TILE = 512
VMEM_LIMIT = 56 * 1024 * 1024


def _mx(a):
    return a.astype(MXU_DTYPE)


def _dot(a, b):
    return jnp.dot(_mx(a), _mx(b), preferred_element_type=F32)


def _dot_nt(a, b):
    return lax.dot_general(_mx(a), _mx(b), (((1,), (1,)), ((), ())), preferred_element_type=F32)


def _split3(x):
    hi = x.astype(MXU_DTYPE)
    r1 = x - hi.astype(F32)
    mid = r1.astype(MXU_DTYPE)
    lo = (r1 - mid.astype(F32)).astype(MXU_DTYPE)
    return hi, mid, lo


def _dot_exact_rhs(x, m):
    hi, mid, lo = _split3(x)
    return (jnp.dot(hi, m, preferred_element_type=F32) + jnp.dot(mid, m, preferred_element_type=F32)
            + jnp.dot(lo, m, preferred_element_type=F32))


def _dot_exact_lhs(m, x):
    hi, mid, lo = _split3(x)
    return (jnp.dot(m, hi, preferred_element_type=F32) + jnp.dot(m, mid, preferred_element_type=F32)
            + jnp.dot(m, lo, preferred_element_type=F32))


def _layer_norm(v, g, b):
    mu = jnp.mean(v, axis=-1, keepdims=True)
    d = v - mu
    var = jnp.mean(d * d, axis=-1, keepdims=True)
    return d * lax.rsqrt(var + LN_EPS) * g + b


def _gelu_tanh(v):
    return 0.5 * v * (1.0 + jnp.tanh(0.7978845608028654 * (v + 0.044715 * (v * v * v))))


def _const_spec(shape):
    nd = len(shape)
    return pl.BlockSpec(shape, lambda *_: (0,) * nd)


def _proj_kernel(x_ref, wu_ref, wz_ref, u_ref, z_ref):
    xb = _mx(x_ref[...])
    u_ref[...] = jnp.dot(xb, wu_ref[...], preferred_element_type=F32)
    z_ref[...] = jnp.dot(xb, wz_ref[...], preferred_element_type=F32)


def _proj(x2, wu, wz):
    n = x2.shape[0]
    tm = min(TOKEN_TILE, n)
    return pl.pallas_call(
        _proj_kernel,
        name="proj",
        grid=(n // tm,),
        in_specs=[pl.BlockSpec((tm, D_MODEL), lambda i: (i, 0)),
                  _const_spec(wu.shape), _const_spec(wz.shape)],
        out_specs=[pl.BlockSpec((tm, S5_WIDTH), lambda i: (i, 0)),
                   pl.BlockSpec((tm, RWKV_IN_W), lambda i: (i, 0))],
        out_shape=[jax.ShapeDtypeStruct((n, S5_WIDTH), F32),
                   jax.ShapeDtypeStruct((n, RWKV_IN_W), F32)],
        compiler_params=pltpu.CompilerParams(dimension_semantics=("arbitrary",),
                                             vmem_limit_bytes=VMEM_LIMIT),
    )(x2, wu, wz)


def _s5_chunk_matrices(lam_re, lam_im, log_dt, b_re, b_im, c_re, c_im, d_skip):
    hp = lax.Precision.HIGHEST
    t = S5_CHUNK
    g = S5_GROUPS
    dt = jnp.exp(log_dt.astype(F32))[..., None]
    lre = lam_re.astype(F32)
    lim = lam_im.astype(F32)
    zr = lre * dt
    zi = lim * dt
    n = jnp.arange(t + 1, dtype=F32)[:, None, None, None]
    mag = jnp.exp(n * zr)
    pr = mag * jnp.cos(n * zi)
    pi = mag * jnp.sin(n * zi)
    den = lre * lre + lim * lim
    lbr = pr[1] - 1.0
    lbi = pi[1]
    qr = (lbr * lre + lbi * lim) / den
    qi = (lbi * lre - lbr * lim) / den
    bre = b_re.astype(F32)
    bim = b_im.astype(F32)
    bbr = qr[..., None] * bre - qi[..., None] * bim
    bbi = qr[..., None] * bim + qi[..., None] * bre
    wr = pr[..., None] * bbr - pi[..., None] * bbi
    wi = pr[..., None] * bbi + pi[..., None] * bbr
    cre = c_re.astype(F32)
    cim = c_im.astype(F32)
    kern = (jnp.einsum('dghp,ndgpk->ndghk', cre, wr[:t], precision=hp)
            - jnp.einsum('dghp,ndgpk->ndghk', cim, wi[:t], precision=hp))
    jj = jnp.arange(t)[:, None]
    ii = jnp.arange(t)[None, :]
    kf = kern[:, 0][jnp.clip(ii - jj, 0, t - 1)]
    kb = kern[:, 1][jnp.clip(jj - ii, 0, t - 1)]
    kf = jnp.where((ii >= jj)[:, :, None, None, None], kf, 0.0)
    kb = jnp.where((jj >= ii)[:, :, None, None, None], kb, 0.0)
    t_mat = jnp.transpose(kf + kb, (2, 0, 4, 1, 3)).reshape(g, S5_CW, S5_CW)

    def rows_jh(a):
        return jnp.transpose(a, (1, 0, 3, 2)).reshape(g, S5_CW, S5_STATE)

    m_in = jnp.concatenate([rows_jh(wr[:t, 0][::-1]), rows_jh(wr[:t, 1]),
                            rows_jh(wi[:t, 0][::-1]), rows_jh(wi[:t, 1])], axis=-1)

    def cols_ih(a):
        return jnp.transpose(a, (1, 3, 0, 2)).reshape(g, S5_STATE, S5_CW)

    pf_r, pf_i = pr[1:, 0][:, :, None, :], pi[1:, 0][:, :, None, :]
    pb_r, pb_i = pr[1:, 1][::-1][:, :, None, :], pi[1:, 1][::-1][:, :, None, :]
    m_out = jnp.concatenate([
        cols_ih(cre[0][None] * pf_r - cim[0][None] * pf_i),
        cols_ih(cre[1][None] * pb_r - cim[1][None] * pb_i),
        cols_ih(-(cre[0][None] * pf_i + cim[0][None] * pf_r)),
        cols_ih(-(cre[1][None] * pb_i + cim[1][None] * pb_r))], axis=1)
    a_step = jnp.stack([jnp.concatenate([pr[t, 0], pr[t, 1]], axis=-1),
                        jnp.concatenate([pi[t, 0], pi[t, 1]], axis=-1)], axis=1)
    d_t = jnp.tile(d_skip.astype(F32).reshape(g, 1, S5_GROUP), (1, t, 1)).reshape(g, 1, S5_CW)
    return _mx(t_mat), _mx(m_in), _mx(m_out), a_step, d_t


def _s5_kernel(u_ref, t_ref, min_ref, mout_ref, a_ref, d_ref, y_ref, sin_ref, xs_ref, *, bsz, n_chunks):
    u = u_ref[0]
    ub = _mx(u)
    sin_ref[...] = jnp.dot(ub, min_ref[0], preferred_element_type=F32)
    a_re = a_ref[0, 0:1, :]
    a_im = a_ref[0, 1:2, :]
    half = S5_STATE
    fwd_lane = lax.broadcasted_iota(jnp.int32, (bsz, 2 * half), 1) < half

    def step(k, carry):
        c_re, c_im = carry
        rk = pl.ds(pl.multiple_of(k * bsz, bsz), bsz)
        rr = pl.ds(pl.multiple_of((n_chunks - 1 - k) * bsz, bsz), bsz)
        xs_ref[rk, 0:half] = c_re[:, 0:half]
        xs_ref[rr, half:2 * half] = c_re[:, half:2 * half]
        xs_ref[rk, 2 * half:3 * half] = c_im[:, 0:half]
        xs_ref[rr, 3 * half:4 * half] = c_im[:, half:2 * half]
        s_re = jnp.where(fwd_lane, sin_ref[rk, 0:2 * half], sin_ref[rr, 0:2 * half])
        s_im = jnp.where(fwd_lane, sin_ref[rk, 2 * half:4 * half], sin_ref[rr, 2 * half:4 * half])
        n_re = a_re * c_re - a_im * c_im + s_re
        n_im = a_re * c_im + a_im * c_re + s_im
        return n_re, n_im

    zero = jnp.zeros((bsz, 2 * half), F32)
    lax.fori_loop(0, n_chunks, step, (zero, zero))
    y = jnp.dot(ub, t_ref[0], preferred_element_type=F32)
    y = y + jnp.dot(_mx(xs_ref[...]), mout_ref[0], preferred_element_type=F32)
    y_ref[0] = y + u * d_ref[0]


def _s5_scan(u_t, mats, bsz, n_chunks):
    t_mat, m_in, m_out, a_step, d_t = mats
    g, rows, cw = u_t.shape
    mat_spec = pl.BlockSpec((1, cw, cw), lambda i: (i, 0, 0))
    return pl.pallas_call(
        functools.partial(_s5_kernel, bsz=bsz, n_chunks=n_chunks),
        name="s5_scan",
        grid=(g,),
        in_specs=[pl.BlockSpec((1, rows, cw), lambda i: (i, 0, 0)), mat_spec, mat_spec, mat_spec,
                  pl.BlockSpec((1, 2, 2 * S5_STATE), lambda i: (i, 0, 0)),
                  pl.BlockSpec((1, 1, cw), lambda i: (i, 0, 0))],
        out_specs=pl.BlockSpec((1, rows, cw), lambda i: (i, 0, 0)),
        out_shape=jax.ShapeDtypeStruct((g, rows, cw), F32),
        scratch_shapes=[pltpu.VMEM((rows, cw), F32), pltpu.VMEM((rows, cw), F32)],
        compiler_params=pltpu.CompilerParams(dimension_semantics=("arbitrary",),
                                             vmem_limit_bytes=VMEM_LIMIT),
    )(u_t, t_mat, m_in, m_out, a_step, d_t)


def _s5_branch(u, mats):
    bsz, seqlen, _ = u.shape
    n_chunks = seqlen // S5_CHUNK
    u_t = u.reshape(bsz, n_chunks, S5_CHUNK, S5_GROUPS, S5_GROUP)
    u_t = jnp.transpose(u_t, (3, 1, 0, 2, 4)).reshape(S5_GROUPS, n_chunks * bsz, S5_CW)
    y_t = _s5_scan(u_t, mats, bsz, n_chunks)
    y = y_t.reshape(S5_GROUPS, n_chunks, bsz, S5_CHUNK, S5_GROUP)
    return jnp.transpose(y, (2, 1, 3, 0, 4)).reshape(bsz, seqlen, S5_WIDTH)


def _rwkv_kernel(z_ref, mu_ref, w0_ref, w2_ref, a0_ref, a2_ref, g2_ref, kk_ref, ka_ref, rk_ref,
                 gng_ref, gnb_ref, o_ref, s_ref, y_ref, bon_ref, *, seqlen):
    t = RWKV_CHUNK
    n_chunks = seqlen // t
    w = RWKV_WIDTH

    row = lax.broadcasted_iota(jnp.int32, (t, 1), 0)
    tri_r = lax.broadcasted_iota(jnp.int32, (t, t), 0)
    tri_c = lax.broadcasted_iota(jnp.int32, (t, t), 1)
    seg_r = lax.broadcasted_iota(jnp.int32, (w, w), 0) // RWKV_HEAD
    seg_c = lax.broadcasted_iota(jnp.int32, (w, w), 1) // RWKV_HEAD
    seg_ones = jnp.where(seg_r == seg_c, 1.0, 0.0).astype(MXU_DTYPE)
    pr = lax.broadcasted_iota(jnp.int32, (PAIR, PAIR), 0)
    pc = lax.broadcasted_iota(jnp.int32, (PAIR, PAIR), 1)
    same_head = (pr // t) == (pc // t)
    eye = jnp.where(pr == pc, 1.0, 0.0).astype(F32)
    first_head = lax.broadcasted_iota(jnp.int32, (t, PAIR), 1) < RWKV_HEAD
    cum_ops = (jnp.where(tri_r >= tri_c, 1.0, 0.0).astype(MXU_DTYPE),
               jnp.where(tri_r <= tri_c, 1.0, 0.0).astype(MXU_DTYPE))
    strict = (same_head & (pr > pc), same_head & (pr < pc))
    incl = (same_head & (pr >= pc), same_head & (pr <= pc))

    mu = mu_ref[...]
    k_k = kk_ref[...]
    k_a = ka_ref[...]
    r_k = rk_ref[...]

    s_ref[...] = jnp.zeros_like(s_ref)
    y_ref[...] = jnp.zeros_like(y_ref)
    bon_ref[...] = jnp.zeros_like(bon_ref)

    def stack(a, p):
        ap = a[:, p * PAIR:(p + 1) * PAIR]
        return jnp.concatenate([jnp.where(first_head, ap, 0.0), jnp.where(first_head, 0.0, ap)], axis=0)

    def prep(ci, d):
        base = pl.multiple_of(ci * t, t)
        rows = pl.ds(base, t)
        zc = z_ref[0, rows, :]
        prev_blk = z_ref[0, pl.ds(pl.multiple_of(jnp.maximum(base - 8, 0), 8), 8), :]
        next_blk = z_ref[0, pl.ds(pl.multiple_of(jnp.minimum(base + t, seqlen - 8), 8), 8), :]
        prev_row = prev_blk[7:8, :] * (ci > 0).astype(F32)
        next_row = next_blk[0:1, :] * (ci < n_chunks - 1).astype(F32)
        prev = jnp.where(row == 0, prev_row, pltpu.roll(zc, 1, 0))
        nxt = jnp.where(row == t - 1, next_row, pltpu.roll(zc, t - 1, 0))
        zs = zc + (0.5 * (prev + nxt) - zc) * mu
        r = zs[:, 0:w]
        k = zs[:, w:2 * w]
        v = zs[:, 2 * w:3 * w]
        w_low = zs[:, 3 * w:3 * w + 2 * DECAY_RANK]
        a_low = zs[:, 3 * w + 2 * DECAY_RANK:3 * w + 2 * DECAY_RANK + 2 * ICLR_RANK]
        if d == 0:
            o_ref[0, rows, :] = _dot(jax.nn.sigmoid(zs[:, 3 * w + 2 * DECAY_RANK + 2 * ICLR_RANK:]), g2_ref[...])

        w_log = w0_ref[d:d + 1, :] + _dot(jnp.tanh(w_low), w2_ref[d])
        w_log = -jnp.logaddexp(-w_log, 0.0) - 0.5
        logw = -jnp.exp(w_log)
        iclr = jax.nn.sigmoid(a0_ref[d:d + 1, :] + _dot(a_low, a2_ref[d]))
        kk = k * k_k
        nrm = jnp.sqrt(_dot_exact_rhs(kk * kk, seg_ones))
        kk = kk / jnp.maximum(nrm, NORM_EPS)
        k_dir = k * (1.0 + (iclr - 1.0) * k_a)
        bon_ref[rows, :] += _dot_exact_rhs(r * k_dir * r_k, seg_ones) * v

        cum = _dot_exact_lhs(cum_ops[d], logw)
        last = cum[t - 1:t, :] if d == 0 else cum[0:1, :]
        e_neg = jnp.exp(-cum)
        b_dir = kk * iclr
        e_end = jnp.exp(last - cum)
        return dict(rows=rows, a=-kk * jnp.exp(cum - logw), r=r * jnp.exp(cum), b=b_dir * e_neg,
                    k=k_dir * e_neg, v=v, be=b_dir * e_end, ke=k_dir * e_end, g_end=jnp.exp(last))

    chains = [(d, p) for d in range(2) for p in range(N_PAIRS)]

    def body(ci, carry):
        ops = (prep(ci, 0), prep(n_chunks - 1 - ci, 1))
        st = {c: {q: _mx(stack(ops[c[0]][q], c[1])) for q in ("a", "r", "b", "k", "v", "be", "ke")}
              for c in chains}
        ar = {c: jnp.concatenate([st[c]["a"], st[c]["r"]], axis=0) for c in chains}
        sc = {c: _dot_nt(ar[c], jnp.concatenate([st[c]["b"], st[c]["k"]], axis=0)) for c in chains}
        l_ab = {c: jnp.where(strict[c[0]], sc[c][0:PAIR, 0:PAIR], 0.0) for c in chains}
        mix = {c: jnp.concatenate([jnp.where(strict[c[0]], sc[c][0:PAIR, PAIR:2 * PAIR], 0.0),
                                   jnp.where(incl[c[0]], sc[c][PAIR:2 * PAIR, PAIR:2 * PAIR], 0.0)], axis=0)
               for c in chains}
        s_old = {c: s_ref[c[0], c[1]] for c in chains}
        x = {c: _dot_nt(ar[c], s_old[c]) + _dot(mix[c], st[c]["v"]) for c in chains}
        m = {c: _dot(l_ab[c], l_ab[c]) for c in chains}
        inv = {c: eye + l_ab[c] for c in chains}
        for _ in range(4):
            mp = {c: _dot(m[c], jnp.concatenate([m[c], inv[c]], axis=1)) for c in chains}
            m = {c: mp[c][:, 0:PAIR] for c in chains}
            inv = {c: inv[c] + mp[c][:, PAIR:2 * PAIR] for c in chains}
        inv = {c: inv[c] + _dot(m[c], inv[c]) for c in chains}
        u = {c: _dot(inv[c], x[c][0:PAIR]) for c in chains}
        m_rb = {c: jnp.where(incl[c[0]], sc[c][PAIR:2 * PAIR, 0:PAIR], 0.0) for c in chains}
        o = {c: x[c][PAIR:2 * PAIR] + _dot(m_rb[c], u[c]) for c in chains}
        for c in chains:
            d, p = c
            lanes = slice(p * PAIR, (p + 1) * PAIR)
            y_ref[ops[d]["rows"], lanes] += o[c][0:t] + o[c][t:2 * t]
            uv_t = jnp.concatenate([u[c].T, st[c]["v"].astype(F32).T], axis=1)
            s_ref[d, p] = (s_old[c] * ops[d]["g_end"][:, lanes]
                           + _dot(uv_t, jnp.concatenate([st[c]["be"], st[c]["ke"]], axis=0)))
        return carry

    lax.fori_loop(0, n_chunks, body, 0)

    blk = min(seqlen, 256)
    gn_g = gng_ref[...]
    gn_b = gnb_ref[...]
    seg_mean = (seg_ones.astype(F32) * (1.0 / RWKV_HEAD)).astype(MXU_DTYPE)

    def norm_body(i, carry):
        rows = pl.ds(pl.multiple_of(i * blk, blk), blk)
        y = y_ref[rows, :]
        dlt = y - _dot_exact_rhs(y, seg_mean)
        var = _dot_exact_rhs(dlt * dlt, seg_mean)
        yn = dlt * lax.rsqrt(var + GN_EPS) * gn_g + gn_b
        o_ref[0, rows, :] = (yn + bon_ref[rows, :]) * o_ref[0, rows, :]
        return carry

    lax.fori_loop(0, seqlen // blk, norm_body, 0)


def _rwkv_branch(z, p):
    bsz, seqlen, _ = z.shape
    consts = [p["mu"], p["w0"], p["w2"], p["a0"], p["a2"], p["g2"], p["k_k"], p["k_a"], p["r_k"],
              p["gn_g"], p["gn_b"]]
    return pl.pallas_call(
        functools.partial(_rwkv_kernel, seqlen=seqlen),
        name="rwkv",
        grid=(bsz,),
        in_specs=[pl.BlockSpec((1, seqlen, RWKV_IN_W), lambda i: (i, 0, 0))]
                 + [_const_spec(c.shape) for c in consts],
        out_specs=pl.BlockSpec((1, seqlen, RWKV_WIDTH), lambda i: (i, 0, 0)),
        out_shape=jax.ShapeDtypeStruct((bsz, seqlen, RWKV_WIDTH), F32),
        scratch_shapes=[pltpu.VMEM((2, N_PAIRS, PAIR, PAIR), F32),
                        pltpu.VMEM((seqlen, RWKV_WIDTH), F32),
                        pltpu.VMEM((seqlen, RWKV_WIDTH), F32)],
        compiler_params=pltpu.CompilerParams(dimension_semantics=("arbitrary",),
                                             vmem_limit_bytes=VMEM_LIMIT),
    )(z, *consts)


def _merge_kernel(x_ref, ys_ref, yr_ref, wg_ref, wglu_ref, bglu_ref, wpa_ref, wpb_ref, wout_ref,
                  g_ref, b_ref, h_ref):
    x = x_ref[...]
    xb = _mx(x)
    act = _gelu_tanh(ys_ref[...])
    out_a = act * jax.nn.sigmoid(_dot(act, wglu_ref[...]) + bglu_ref[...])
    gate_a = jax.nn.sigmoid(jnp.dot(xb, wg_ref[:, 0:D_MODEL], preferred_element_type=F32))
    merged = gate_a * _dot(out_a, wpa_ref[...])
    gate_b = jax.nn.sigmoid(jnp.dot(xb, wg_ref[:, D_MODEL:2 * D_MODEL], preferred_element_type=F32))
    merged = merged + gate_b * _dot(yr_ref[...], wpb_ref[...])
    h_ref[...] = _layer_norm(ALPHA * x + _dot(merged, wout_ref[...]), g_ref[...], b_ref[...])


def _merge(x2, ys, yr, p):
    n = x2.shape[0]
    tm = min(TOKEN_TILE, n)
    consts = [p["w_gates"], p["w_glu"], p["b_glu"], p["w_pa"], p["w_pb"], p["w_out"], p["ln1_g"], p["ln1_b"]]
    return pl.pallas_call(
        _merge_kernel,
        name="merge",
        grid=(n // tm,),
        in_specs=[pl.BlockSpec((tm, D_MODEL), lambda i: (i, 0)),
                  pl.BlockSpec((tm, S5_WIDTH), lambda i: (i, 0)),
                  pl.BlockSpec((tm, RWKV_WIDTH), lambda i: (i, 0))]
                 + [_const_spec(c.shape) for c in consts],
        out_specs=pl.BlockSpec((tm, D_MODEL), lambda i: (i, 0)),
        out_shape=jax.ShapeDtypeStruct((n, D_MODEL), F32),
        compiler_params=pltpu.CompilerParams(dimension_semantics=("arbitrary",),
                                             vmem_limit_bytes=VMEM_LIMIT),
    )(x2, ys, yr, *consts)


def _ffn_kernel(h_ref, wg_ref, wu_ref, wd_ref, g_ref, b_ref, y_ref):
    h = h_ref[...]
    hb = _mx(h)
    half = D_FF // 2
    acc = ALPHA * h
    for c in range(2):
        cols = slice(c * half, (c + 1) * half)
        gate = jnp.dot(hb, wg_ref[:, cols], preferred_element_type=F32)
        up = jnp.dot(hb, wu_ref[:, cols], preferred_element_type=F32)
        acc = acc + _dot(gate * jax.nn.sigmoid(gate) * up, wd_ref[cols, :])
    y_ref[...] = _layer_norm(acc, g_ref[...], b_ref[...])


def _ffn(h2, p):
    n = h2.shape[0]
    tm = min(TOKEN_TILE, n)
    consts = [p["w_gate"], p["w_up"], p["w_down"], p["ln2_g"], p["ln2_b"]]
    return pl.pallas_call(
        _ffn_kernel,
        name="ffn",
        grid=(n // tm,),
        in_specs=[pl.BlockSpec((tm, D_MODEL), lambda i: (i, 0))] + [_const_spec(c.shape) for c in consts],
        out_specs=pl.BlockSpec((tm, D_MODEL), lambda i: (i, 0)),
        out_shape=jax.ShapeDtypeStruct((n, D_MODEL), F32),
        compiler_params=pltpu.CompilerParams(dimension_semantics=("arbitrary",),
                                             vmem_limit_bytes=VMEM_LIMIT),
    )(h2, *consts)


def _pad_rank(w2):
    z = jnp.zeros_like(w2[0])
    return jnp.stack([jnp.concatenate([w2[0], z], axis=0), jnp.concatenate([z, w2[1]], axis=0)])


def _row(v):
    return v.astype(F32).reshape(1, -1)


def _encoder_layer(x, p):
    bsz, seqlen, _ = x.shape
    x2 = x.reshape(bsz * seqlen, D_MODEL)
    u, z = _proj(x2, p["w_u"], p["w_z"])
    ys = _s5_branch(u.reshape(bsz, seqlen, S5_WIDTH), p["s5_mats"]).reshape(bsz * seqlen, S5_WIDTH)
    yr = _rwkv_branch(z.reshape(bsz, seqlen, RWKV_IN_W), p).reshape(bsz * seqlen, RWKV_WIDTH)
    h = _merge(x2, ys, yr, p)
    return _ffn(h, p).reshape(bsz, seqlen, D_MODEL)


def kernel(x_prompt, x_sample, w_in, s5_lam_re, s5_lam_im, s5_log_dt, s5_b_re, s5_b_im, s5_c_re, s5_c_im,
           s5_d, s5_w_glu, s5_b_glu, rwkv_mu, rwkv_w0, rwkv_w2, rwkv_a0, rwkv_a2, rwkv_g2, rwkv_k_k,
           rwkv_k_a, rwkv_r_k, rwkv_gn_g, rwkv_gn_b, w_pa, w_pb, w_out, ln1_g, ln1_b, w_gate, w_up,
           w_down, ln2_g, ln2_b):
    y_prompt, y_sample = x_prompt, x_sample
    for layer in range(w_in.shape[0]):
        wl = w_in[layer]
        gate_start = S5_WIDTH + RWKV_IN_W
        p = {
            "w_u": _mx(wl[:, :S5_WIDTH]),
            "w_z": _mx(wl[:, S5_WIDTH:gate_start]),
            "w_gates": _mx(wl[:, gate_start:]),
            "s5_mats": _s5_chunk_matrices(s5_lam_re[layer], s5_lam_im[layer], s5_log_dt[layer],
                                          s5_b_re[layer], s5_b_im[layer], s5_c_re[layer], s5_c_im[layer],
                                          s5_d[layer]),
            "w_glu": _mx(s5_w_glu[layer]), "b_glu": _row(s5_b_glu[layer]),
            "mu": _row(rwkv_mu[layer]), "w0": rwkv_w0[layer].astype(F32),
            "w2": _mx(_pad_rank(rwkv_w2[layer])), "a0": rwkv_a0[layer].astype(F32),
            "a2": _mx(_pad_rank(rwkv_a2[layer])), "g2": _mx(rwkv_g2[layer]),
            "k_k": _row(rwkv_k_k[layer]), "k_a": _row(rwkv_k_a[layer]), "r_k": _row(rwkv_r_k[layer]),
            "gn_g": _row(rwkv_gn_g[layer]), "gn_b": _row(rwkv_gn_b[layer]),
            "w_pa": _mx(w_pa[layer]), "w_pb": _mx(w_pb[layer]), "w_out": _mx(w_out[layer]),
            "ln1_g": _row(ln1_g[layer]), "ln1_b": _row(ln1_b[layer]),
            "w_gate": _mx(w_gate[layer]), "w_up": _mx(w_up[layer]), "w_down": _mx(w_down[layer]),
            "ln2_g": _row(ln2_g[layer]), "ln2_b": _row(ln2_b[layer]),
        }
        y_prompt = _encoder_layer(y_prompt, p)
        y_sample = _encoder_layer(y_sample, p)
    return (y_prompt, y_sample)
```

```python
import functools

import jax
import jax.numpy as jnp
from jax import lax
from jax.experimental import pallas as pl
from jax.experimental.pallas import tpu as pltpu

F32 = jnp.float32
MXU_DTYPE = jnp.bfloat16

D_MODEL = 1024
S5_WIDTH = 512
S5_GROUP = 16
S5_GROUPS = S5_WIDTH // S5_GROUP
S5_STATE = 64
RWKV_WIDTH = 512
RWKV_HEAD = 64
DECAY_RANK = 64
ICLR_RANK = 64
GATE_RANK = 128
RWKV_IN_W = 3 * RWKV_WIDTH + 2 * DECAY_RANK + 2 * ICLR_RANK + GATE_RANK
D_FF = 2816
DEPTH = 1
ALPHA = (2.0 * DEPTH) ** 0.25
LN_EPS = 1e-5
GN_EPS = 64e-5
NORM_EPS = 1e-12

S5_CHUNK = 16
S5_CW = S5_CHUNK * S5_GROUP
S5_SLOT = 128
S5_LANES = 128
S5_SEQ_TILE = 8
RWKV_CHUNK = 64
PAIR = 2 * RWKV_HEAD
N_PAIRS = RWKV_WIDTH // PAIR
TOKEN_TILE = 512
VMEM_LIMIT = 56 * 1024 * 1024


def _mx(a):
    return a.astype(MXU_DTYPE)


def _dot(a, b):
    return jnp.dot(_mx(a), _mx(b), preferred_element_type=F32)


def _dot_nt(a, b):
    return lax.dot_general(_mx(a), _mx(b), (((1,), (1,)), ((), ())), preferred_element_type=F32)


def _split2(x):
    hi = x.astype(MXU_DTYPE)
    return hi, (x - hi.astype(F32)).astype(MXU_DTYPE)


def _dot_exact_rhs(x, m):
    hi, lo = _split2(x)
    return jnp.dot(hi, m, preferred_element_type=F32) + jnp.dot(lo, m, preferred_element_type=F32)


def _dot_exact_lhs(m, x):
    hi, lo = _split2(x)
    return jnp.dot(m, hi, preferred_element_type=F32) + jnp.dot(m, lo, preferred_element_type=F32)


def _layer_norm(v, g, b):
    mu = jnp.mean(v, axis=-1, keepdims=True)
    d = v - mu
    var = jnp.mean(d * d, axis=-1, keepdims=True)
    return d * lax.rsqrt(var + LN_EPS) * g + b


def _gelu_tanh(v):
    return 0.5 * v * (1.0 + jnp.tanh(0.7978845608028654 * (v + 0.044715 * (v * v * v))))


def _const_spec(shape):
    nd = len(shape)
    return pl.BlockSpec(shape, lambda *_: (0,) * nd)


def _proj_kernel(x_ref, xp_ref, xn_ref, wu_ref, wz_ref, mu_ref, u_ref, z_ref, *, seqlen):
    tm = x_ref.shape[0]
    x = x_ref[...]
    u_ref[...] = jnp.dot(_mx(x), wu_ref[...], preferred_element_type=F32)
    x_ext = jnp.concatenate([xp_ref[...], x, xn_ref[...]], axis=0)
    z_ext = jnp.dot(_mx(x_ext), wz_ref[...], preferred_element_type=F32)
    first = pl.program_id(0) * tm
    row = lax.broadcasted_iota(jnp.int32, (tm, 1), 0)
    has_prev = (row > 0) | (first % seqlen != 0)
    has_next = (row < tm - 1) | ((first + tm) % seqlen != 0)
    z = z_ext[8:8 + tm]
    prev = jnp.where(has_prev, pltpu.roll(z_ext, 1, 0)[8:8 + tm], 0.0)
    nxt = jnp.where(has_next, pltpu.roll(z_ext, tm + 15, 0)[8:8 + tm], 0.0)
    z_ref[...] = z + (0.5 * (prev + nxt) - z) * mu_ref[...]


def _proj(x2, wu, wz, mu, seqlen):
    n = x2.shape[0]
    tm = min(TOKEN_TILE, seqlen)
    blocks8 = tm // 8
    last8 = n // 8 - 1
    return pl.pallas_call(
        functools.partial(_proj_kernel, seqlen=seqlen),
        name="proj",
        grid=(n // tm,),
        in_specs=[pl.BlockSpec((tm, D_MODEL), lambda i: (i, 0)),
                  pl.BlockSpec((8, D_MODEL), lambda i: (jnp.maximum(i * blocks8 - 1, 0), 0)),
                  pl.BlockSpec((8, D_MODEL), lambda i: (jnp.minimum((i + 1) * blocks8, last8), 0)),
                  _const_spec(wu.shape), _const_spec(wz.shape), _const_spec(mu.shape)],
        out_specs=[pl.BlockSpec((tm, S5_WIDTH), lambda i: (i, 0)),
                   pl.BlockSpec((tm, RWKV_IN_W), lambda i: (i, 0))],
        out_shape=[jax.ShapeDtypeStruct((n, S5_WIDTH), F32),
                   jax.ShapeDtypeStruct((n, RWKV_IN_W), F32)],
        compiler_params=pltpu.CompilerParams(dimension_semantics=("arbitrary",),
                                             vmem_limit_bytes=VMEM_LIMIT),
    )(x2, x2, x2, wu, wz, mu)


def _s5_chunk_matrices(lam_re, lam_im, log_dt, b_re, b_im, c_re, c_im, d_skip):
    hp = lax.Precision.HIGHEST
    t = S5_CHUNK
    g = S5_GROUPS
    dt = jnp.exp(log_dt.astype(F32))[..., None]
    lre = lam_re.astype(F32)
    lim = lam_im.astype(F32)
    zr = lre * dt
    zi = lim * dt
    n = jnp.arange(t + 1, dtype=F32)[:, None, None, None]
    mag = jnp.exp(n * zr)
    pr = mag * jnp.cos(n * zi)
    pi = mag * jnp.sin(n * zi)
    den = lre * lre + lim * lim
    lbr = pr[1] - 1.0
    lbi = pi[1]
    qr = (lbr * lre + lbi * lim) / den
    qi = (lbi * lre - lbr * lim) / den
    bre = b_re.astype(F32)
    bim = b_im.astype(F32)
    bbr = qr[..., None] * bre - qi[..., None] * bim
    bbi = qr[..., None] * bim + qi[..., None] * bre
    wr = pr[..., None] * bbr - pi[..., None] * bbi
    wi = pr[..., None] * bbi + pi[..., None] * bbr
    cre = c_re.astype(F32)
    cim = c_im.astype(F32)
    kern = (jnp.einsum('dghp,ndgpk->ndghk', cre, wr[:t], precision=hp)
            - jnp.einsum('dghp,ndgpk->ndghk', cim, wi[:t], precision=hp))
    jj = jnp.arange(t)[:, None]
    ii = jnp.arange(t)[None, :]
    kf = kern[:, 0][jnp.clip(ii - jj, 0, t - 1)]
    kb = kern[:, 1][jnp.clip(jj - ii, 0, t - 1)]
    kf = jnp.where((ii >= jj)[:, :, None, None, None], kf, 0.0)
    kb = jnp.where((jj >= ii)[:, :, None, None, None], kb, 0.0)
    t_mat = jnp.transpose(kf + kb, (2, 0, 4, 1, 3)).reshape(g, S5_CW, S5_CW)

    def rows_jh(a):
        a = jnp.transpose(a, (1, 0, 3, 2)).reshape(g, S5_CW, S5_STATE)
        return jnp.pad(a, ((0, 0), (0, 0), (0, S5_SLOT - S5_STATE)))

    m_in = jnp.concatenate([rows_jh(wr[:t, 0][::-1]), rows_jh(wi[:t, 0][::-1]),
                            rows_jh(wr[:t, 1]), rows_jh(wi[:t, 1])], axis=-1)

    def cols_ih(a):
        a = jnp.transpose(a, (1, 3, 0, 2)).reshape(g, S5_STATE, S5_CW)
        return jnp.pad(a, ((0, 0), (0, S5_SLOT - S5_STATE), (0, 0)))

    pf_r, pf_i = pr[1:, 0][:, :, None, :], pi[1:, 0][:, :, None, :]
    pb_r, pb_i = pr[1:, 1][::-1][:, :, None, :], pi[1:, 1][::-1][:, :, None, :]
    m_out = jnp.concatenate([
        cols_ih(cre[0][None] * pf_r - cim[0][None] * pf_i),
        cols_ih(-(cre[0][None] * pf_i + cim[0][None] * pf_r)),
        cols_ih(cre[1][None] * pb_r - cim[1][None] * pb_i),
        cols_ih(-(cre[1][None] * pb_i + cim[1][None] * pb_r))], axis=1)
    a_step = jnp.pad(jnp.stack([pr[t, 0], pi[t, 0], pr[t, 1], pi[t, 1]], axis=1),
                     ((0, 0), (0, 0), (0, S5_SLOT - S5_STATE)))
    return _mx(t_mat), _mx(m_in), _mx(m_out), a_step, d_skip.astype(F32).reshape(1, S5_WIDTH)


def _block_transpose8(xs):
    xs = list(xs)
    lane_block = lax.broadcasted_iota(jnp.int32, (1, S5_LANES), 1) // S5_GROUP
    for s in (4, 2, 1):
        keep = (lane_block & s) == 0
        for a in range(8):
            if a & s:
                continue
            lo, hi = xs[a], xs[a | s]
            xs[a] = jnp.where(keep, lo, pltpu.roll(hi, S5_GROUP * s, 1))
            xs[a | s] = jnp.where(keep, pltpu.roll(lo, S5_LANES - S5_GROUP * s, 1), hi)
    return xs


def _s5_kernel(u_ref, t_ref, min_ref, mout_ref, a_ref, d_ref, y_ref, q_ref, yg_ref, sx_ref, *, n_chunks):
    bt = S5_SEQ_TILE
    rows = bt * n_chunks
    n_groups = S5_LANES // S5_GROUP
    slot = S5_SLOT

    for v in range(2):
        pieces = [u_ref[pl.ds(8 * v + a, rows, stride=S5_CHUNK), :] for a in range(8)]
        for gi, q in enumerate(_block_transpose8(pieces)):
            q_ref[gi, :, v * S5_LANES:(v + 1) * S5_LANES] = _mx(q)

    zero = jnp.zeros((bt, slot), F32)
    for gi in range(n_groups):
        q = q_ref[gi]
        s_in = jnp.dot(q, min_ref[gi], preferred_element_type=F32)
        for c in range(4):
            sx_ref[c] = s_in[:, c * slot:(c + 1) * slot]
        a = a_ref[gi]
        af_re, af_im, ab_re, ab_im = a[0:1], a[1:2], a[2:3], a[3:4]

        def step(k, carry):
            f_re, f_im, b_re, b_im = carry
            rk = pl.ds(k, bt, stride=n_chunks)
            rr = pl.ds(n_chunks - 1 - k, bt, stride=n_chunks)
            sf_re = sx_ref[0, rk, :]
            sf_im = sx_ref[1, rk, :]
            sb_re = sx_ref[2, rr, :]
            sb_im = sx_ref[3, rr, :]
            sx_ref[0, rk, :] = f_re
            sx_ref[1, rk, :] = f_im
            sx_ref[2, rr, :] = b_re
            sx_ref[3, rr, :] = b_im
            return (af_re * f_re - af_im * f_im + sf_re, af_re * f_im + af_im * f_re + sf_im,
                    ab_re * b_re - ab_im * b_im + sb_re, ab_re * b_im + ab_im * b_re + sb_im)

        lax.fori_loop(0, n_chunks, step, (zero, zero, zero, zero))
        states = jnp.concatenate([_mx(sx_ref[c]) for c in range(4)], axis=1)
        yg_ref[gi] = (jnp.dot(q, t_ref[gi], preferred_element_type=F32)
                      + jnp.dot(states, mout_ref[gi], preferred_element_type=F32))

    d = d_ref[...]
    for v in range(2):
        halves = [yg_ref[gi, :, v * S5_LANES:(v + 1) * S5_LANES] for gi in range(n_groups)]
        for a, y in enumerate(_block_transpose8(halves)):
            token_rows = pl.ds(8 * v + a, rows, stride=S5_CHUNK)
            y_ref[token_rows, :] = y + d * u_ref[token_rows, :]


def _s5_branch(u2, mats, seqlen):
    t_mat, m_in, m_out, a_step, d_row = mats
    n = u2.shape[0]
    n_chunks = seqlen // S5_CHUNK
    n_groups = S5_LANES // S5_GROUP
    tile_rows = S5_SEQ_TILE * seqlen
    pairs = S5_SEQ_TILE * n_chunks
    io_spec = pl.BlockSpec((tile_rows, S5_LANES), lambda s, b: (b, s), pipeline_mode=pl.Buffered(1))
    return pl.pallas_call(
        functools.partial(_s5_kernel, n_chunks=n_chunks),
        name="s5_scan",
        grid=(S5_WIDTH // S5_LANES, n // tile_rows),
        in_specs=[io_spec,
                  pl.BlockSpec((n_groups, S5_CW, S5_CW), lambda s, b: (s, 0, 0)),
                  pl.BlockSpec((n_groups, S5_CW, 4 * S5_SLOT), lambda s, b: (s, 0, 0)),
                  pl.BlockSpec((n_groups, 4 * S5_SLOT, S5_CW), lambda s, b: (s, 0, 0)),
                  pl.BlockSpec((n_groups, 4, S5_SLOT), lambda s, b: (s, 0, 0)),
                  pl.BlockSpec((1, S5_LANES), lambda s, b: (0, s))],
        out_specs=io_spec,
        out_shape=jax.ShapeDtypeStruct((n, S5_WIDTH), F32),
        scratch_shapes=[pltpu.VMEM((n_groups, pairs, S5_CW), MXU_DTYPE),
                        pltpu.VMEM((n_groups, pairs, S5_CW), F32),
                        pltpu.VMEM((4, pairs, S5_SLOT), F32)],
        compiler_params=pltpu.CompilerParams(dimension_semantics=("arbitrary", "arbitrary"),
                                             vmem_limit_bytes=VMEM_LIMIT),
    )(u2, t_mat, m_in, m_out, a_step, d_row)


def _rwkv_kernel(z_ref, w0_ref, w2_ref, a0_ref, a2_ref, g2_ref, kk_ref, ka_ref, rk_ref,
                 gng_ref, gnb_ref, o_ref, s_ref, y_ref, bon_ref, *, seqlen):
    t = RWKV_CHUNK
    n_chunks = seqlen // t
    w = RWKV_WIDTH

    tri_r = lax.broadcasted_iota(jnp.int32, (t, t), 0)
    tri_c = lax.broadcasted_iota(jnp.int32, (t, t), 1)
    seg_r = lax.broadcasted_iota(jnp.int32, (w, w), 0) // RWKV_HEAD
    seg_c = lax.broadcasted_iota(jnp.int32, (w, w), 1) // RWKV_HEAD
    seg_ones = jnp.where(seg_r == seg_c, 1.0, 0.0).astype(MXU_DTYPE)
    pr = lax.broadcasted_iota(jnp.int32, (PAIR, PAIR), 0)
    pc = lax.broadcasted_iota(jnp.int32, (PAIR, PAIR), 1)
    same_head = (pr // t) == (pc // t)
    eye = jnp.where(pr == pc, 1.0, 0.0).astype(F32)
    first_head = lax.broadcasted_iota(jnp.int32, (t, PAIR), 1) < RWKV_HEAD
    cum_ops = (jnp.where(tri_r >= tri_c, 1.0, 0.0).astype(MXU_DTYPE),
               jnp.where(tri_r <= tri_c, 1.0, 0.0).astype(MXU_DTYPE))
    strict = (same_head & (pr > pc), same_head & (pr < pc))
    incl = (same_head & (pr >= pc), same_head & (pr <= pc))

    k_k = kk_ref[...]
    k_a = ka_ref[...]
    r_k = rk_ref[...]

    s_ref[...] = jnp.zeros_like(s_ref)
    y_ref[...] = jnp.zeros_like(y_ref)
    bon_ref[...] = jnp.zeros_like(bon_ref)

    def stack(a, p):
        ap = a[:, p * PAIR:(p + 1) * PAIR]
        return jnp.concatenate([jnp.where(first_head, ap, 0.0), jnp.where(first_head, 0.0, ap)], axis=0)

    def prep(ci, d):
        rows = pl.ds(pl.multiple_of(ci * t, t), t)
        zs = z_ref[0, rows, :]
        r = zs[:, 0:w]
        k = zs[:, w:2 * w]
        v = zs[:, 2 * w:3 * w]
        w_low = zs[:, 3 * w:3 * w + 2 * DECAY_RANK]
        a_low = zs[:, 3 * w + 2 * DECAY_RANK:3 * w + 2 * DECAY_RANK + 2 * ICLR_RANK]
        if d == 0:
            o_ref[0, rows, :] = _dot(jax.nn.sigmoid(zs[:, 3 * w + 2 * DECAY_RANK + 2 * ICLR_RANK:]), g2_ref[...])

        w_log = w0_ref[d:d + 1, :] + _dot(jnp.tanh(w_low), w2_ref[d])
        w_log = -jnp.logaddexp(-w_log, 0.0) - 0.5
        logw = -jnp.exp(w_log)
        iclr = jax.nn.sigmoid(a0_ref[d:d + 1, :] + _dot(a_low, a2_ref[d]))
        kk = k * k_k
        nrm = jnp.sqrt(_dot(kk * kk, seg_ones))
        kk = kk / jnp.maximum(nrm, NORM_EPS)
        k_dir = k * (1.0 + (iclr - 1.0) * k_a)
        bon_ref[rows, :] += _dot_exact_rhs(r * k_dir * r_k, seg_ones) * v

        cum = _dot_exact_lhs(cum_ops[d], logw)
        last = cum[t - 1:t, :] if d == 0 else cum[0:1, :]
        e_neg = jnp.exp(-cum)
        b_dir = kk * iclr
        e_end = jnp.exp(last - cum)
        return dict(rows=rows, a=-kk * jnp.exp(cum - logw), r=r * jnp.exp(cum), b=b_dir * e_neg,
                    k=k_dir * e_neg, v=v, be=b_dir * e_end, ke=k_dir * e_end, g_end=jnp.exp(last))

    chains = [(d, p) for d in range(2) for p in range(N_PAIRS)]

    def body(ci, carry):
        ops = (prep(ci, 0), prep(n_chunks - 1 - ci, 1))
        st = {c: {q: _mx(stack(ops[c[0]][q], c[1])) for q in ("a", "r", "b", "k", "v", "be", "ke")}
              for c in chains}
        ar = {c: jnp.concatenate([st[c]["a"], st[c]["r"]], axis=0) for c in chains}
        sc = {c: _dot_nt(ar[c], jnp.concatenate([st[c]["b"], st[c]["k"]], axis=0)) for c in chains}
        l_ab = {c: jnp.where(strict[c[0]], sc[c][0:PAIR, 0:PAIR], 0.0) for c in chains}
        mix = {c: jnp.concatenate([jnp.where(strict[c[0]], sc[c][0:PAIR, PAIR:2 * PAIR], 0.0),
                                   jnp.where(incl[c[0]], sc[c][PAIR:2 * PAIR, PAIR:2 * PAIR], 0.0)], axis=0)
               for c in chains}
        s_old = {c: s_ref[c[0], c[1]] for c in chains}
        x = {c: _dot(jnp.concatenate([ar[c], _mx(mix[c])], axis=1),
                     jnp.concatenate([_mx(s_old[c].T), st[c]["v"]], axis=0)) for c in chains}
        m = {c: _dot(l_ab[c], l_ab[c]) for c in chains}
        inv = {c: eye + l_ab[c] for c in chains}
        for _ in range(4):
            mp = {c: _dot(m[c], jnp.concatenate([m[c], inv[c]], axis=1)) for c in chains}
            m = {c: mp[c][:, 0:PAIR] for c in chains}
            inv = {c: inv[c] + mp[c][:, PAIR:2 * PAIR] for c in chains}
        inv = {c: inv[c] + _dot(m[c], inv[c]) for c in chains}
        u = {c: _dot(inv[c], x[c][0:PAIR]) for c in chains}
        m_rb = {c: jnp.where(incl[c[0]], sc[c][PAIR:2 * PAIR, 0:PAIR], 0.0) for c in chains}
        o = {c: x[c][PAIR:2 * PAIR] + _dot(m_rb[c], u[c]) for c in chains}
        for c in chains:
            d, p = c
            lanes = slice(p * PAIR, (p + 1) * PAIR)
            y_ref[ops[d]["rows"], lanes] += o[c][0:t] + o[c][t:2 * t]
            uv_t = jnp.concatenate([u[c].T, st[c]["v"].astype(F32).T], axis=1)
            s_ref[d, p] = (s_old[c] * ops[d]["g_end"][:, lanes]
                           + _dot(uv_t, jnp.concatenate([st[c]["be"], st[c]["ke"]], axis=0)))
        return carry

    lax.fori_loop(0, n_chunks, body, 0)

    blk = min(seqlen, 256)
    gn_g = gng_ref[...]
    gn_b = gnb_ref[...]
    seg_mean = (seg_ones.astype(F32) * (1.0 / RWKV_HEAD)).astype(MXU_DTYPE)

    def norm_body(i, carry):
        rows = pl.ds(pl.multiple_of(i * blk, blk), blk)
        y = y_ref[rows, :]
        dlt = y - _dot_exact_rhs(y, seg_mean)
        var = _dot(dlt * dlt, seg_mean)
        yn = dlt * lax.rsqrt(var + GN_EPS) * gn_g + gn_b
        o_ref[0, rows, :] = (yn + bon_ref[rows, :]) * o_ref[0, rows, :]
        return carry

    lax.fori_loop(0, seqlen // blk, norm_body, 0)


def _rwkv_branch(z, p):
    bsz, seqlen, _ = z.shape
    consts = [p["w0"], p["w2"], p["a0"], p["a2"], p["g2"], p["k_k"], p["k_a"], p["r_k"],
              p["gn_g"], p["gn_b"]]
    return pl.pallas_call(
        functools.partial(_rwkv_kernel, seqlen=seqlen),
        name="rwkv",
        grid=(bsz,),
        in_specs=[pl.BlockSpec((1, seqlen, RWKV_IN_W), lambda i: (i, 0, 0))]
                 + [_const_spec(c.shape) for c in consts],
        out_specs=pl.BlockSpec((1, seqlen, RWKV_WIDTH), lambda i: (i, 0, 0)),
        out_shape=jax.ShapeDtypeStruct((bsz, seqlen, RWKV_WIDTH), F32),
        scratch_shapes=[pltpu.VMEM((2, N_PAIRS, PAIR, PAIR), F32),
                        pltpu.VMEM((seqlen, RWKV_WIDTH), F32),
                        pltpu.VMEM((seqlen, RWKV_WIDTH), F32)],
        compiler_params=pltpu.CompilerParams(dimension_semantics=("arbitrary",),
                                             vmem_limit_bytes=VMEM_LIMIT),
    )(z, *consts)


def _merge_kernel(x_ref, ys_ref, yr_ref, wg_ref, wglu_ref, bglu_ref, wpa_ref, wpb_ref, wout_ref,
                  g_ref, b_ref, h_ref):
    x = x_ref[...]
    xb = _mx(x)
    act = _gelu_tanh(ys_ref[...])
    out_a = act * jax.nn.sigmoid(_dot(act, wglu_ref[...]) + bglu_ref[...])
    gate_a = jax.nn.sigmoid(jnp.dot(xb, wg_ref[:, 0:D_MODEL], preferred_element_type=F32))
    merged = gate_a * _dot(out_a, wpa_ref[...])
    gate_b = jax.nn.sigmoid(jnp.dot(xb, wg_ref[:, D_MODEL:2 * D_MODEL], preferred_element_type=F32))
    merged = merged + gate_b * _dot(yr_ref[...], wpb_ref[...])
    h_ref[...] = _layer_norm(ALPHA * x + _dot(merged, wout_ref[...]), g_ref[...], b_ref[...])


def _merge(x2, ys, yr, p):
    n = x2.shape[0]
    tm = min(TOKEN_TILE, n)
    consts = [p["w_gates"], p["w_glu"], p["b_glu"], p["w_pa"], p["w_pb"], p["w_out"], p["ln1_g"], p["ln1_b"]]
    return pl.pallas_call(
        _merge_kernel,
        name="merge",
        grid=(n // tm,),
        in_specs=[pl.BlockSpec((tm, D_MODEL), lambda i: (i, 0)),
                  pl.BlockSpec((tm, S5_WIDTH), lambda i: (i, 0)),
                  pl.BlockSpec((tm, RWKV_WIDTH), lambda i: (i, 0))]
                 + [_const_spec(c.shape) for c in consts],
        out_specs=pl.BlockSpec((tm, D_MODEL), lambda i: (i, 0)),
        out_shape=jax.ShapeDtypeStruct((n, D_MODEL), F32),
        compiler_params=pltpu.CompilerParams(dimension_semantics=("arbitrary",),
                                             vmem_limit_bytes=VMEM_LIMIT),
    )(x2, ys, yr, *consts)


def _ffn_kernel(h_ref, wg_ref, wu_ref, wd_ref, g_ref, b_ref, y_ref):
    h = h_ref[...]
    hb = _mx(h)
    half = D_FF // 2
    acc = ALPHA * h
    for c in range(2):
        cols = slice(c * half, (c + 1) * half)
        gate = jnp.dot(hb, wg_ref[:, cols], preferred_element_type=F32)
        up = jnp.dot(hb, wu_ref[:, cols], preferred_element_type=F32)
        acc = acc + _dot(gate * jax.nn.sigmoid(gate) * up, wd_ref[cols, :])
    y_ref[...] = _layer_norm(acc, g_ref[...], b_ref[...])


def _ffn(h2, p):
    n = h2.shape[0]
    tm = min(TOKEN_TILE, n)
    consts = [p["w_gate"], p["w_up"], p["w_down"], p["ln2_g"], p["ln2_b"]]
    return pl.pallas_call(
        _ffn_kernel,
        name="ffn",
        grid=(n // tm,),
        in_specs=[pl.BlockSpec((tm, D_MODEL), lambda i: (i, 0))] + [_const_spec(c.shape) for c in consts],
        out_specs=pl.BlockSpec((tm, D_MODEL), lambda i: (i, 0)),
        out_shape=jax.ShapeDtypeStruct((n, D_MODEL), F32),
        compiler_params=pltpu.CompilerParams(dimension_semantics=("arbitrary",),
                                             vmem_limit_bytes=VMEM_LIMIT),
    )(h2, *consts)


def _pad_rank(w2):
    z = jnp.zeros_like(w2[0])
    return jnp.stack([jnp.concatenate([w2[0], z], axis=0), jnp.concatenate([z, w2[1]], axis=0)])


def _row(v):
    return v.astype(F32).reshape(1, -1)


def _encoder_layer(x, p):
    bsz, seqlen, _ = x.shape
    x2 = x.reshape(bsz * seqlen, D_MODEL)
    u, z = _proj(x2, p["w_u"], p["w_z"], p["mu"], seqlen)
    ys = _s5_branch(u, p["s5_mats"], seqlen)
    yr = _rwkv_branch(z.reshape(bsz, seqlen, RWKV_IN_W), p).reshape(bsz * seqlen, RWKV_WIDTH)
    h = _merge(x2, ys, yr, p)
    return _ffn(h, p).reshape(bsz, seqlen, D_MODEL)


def kernel(x_prompt, x_sample, w_in, s5_lam_re, s5_lam_im, s5_log_dt, s5_b_re, s5_b_im, s5_c_re, s5_c_im,
           s5_d, s5_w_glu, s5_b_glu, rwkv_mu, rwkv_w0, rwkv_w2, rwkv_a0, rwkv_a2, rwkv_g2, rwkv_k_k,
           rwkv_k_a, rwkv_r_k, rwkv_gn_g, rwkv_gn_b, w_pa, w_pb, w_out, ln1_g, ln1_b, w_gate, w_up,
           w_down, ln2_g, ln2_b):
    y_prompt, y_sample = x_prompt, x_sample
    for layer in range(w_in.shape[0]):
        wl = w_in[layer]
        gate_start = S5_WIDTH + RWKV_IN_W
        p = {
            "w_u": _mx(wl[:, :S5_WIDTH]),
            "w_z": _mx(wl[:, S5_WIDTH:gate_start]),
            "w_gates": _mx(wl[:, gate_start:]),
            "s5_mats": _s5_chunk_matrices(s5_lam_re[layer], s5_lam_im[layer], s5_log_dt[layer],
                                          s5_b_re[layer], s5_b_im[layer], s5_c_re[layer], s5_c_im[layer],
                                          s5_d[layer]),
            "w_glu": _mx(s5_w_glu[layer]), "b_glu": _row(s5_b_glu[layer]),
            "mu": _row(rwkv_mu[layer]), "w0": rwkv_w0[layer].astype(F32),
            "w2": _mx(_pad_rank(rwkv_w2[layer])), "a0": rwkv_a0[layer].astype(F32),
            "a2": _mx(_pad_rank(rwkv_a2[layer])), "g2": _mx(rwkv_g2[layer]),
            "k_k": _row(rwkv_k_k[layer]), "k_a": _row(rwkv_k_a[layer]), "r_k": _row(rwkv_r_k[layer]),
            "gn_g": _row(rwkv_gn_g[layer]), "gn_b": _row(rwkv_gn_b[layer]),
            "w_pa": _mx(w_pa[layer]), "w_pb": _mx(w_pb[layer]), "w_out": _mx(w_out[layer]),
            "ln1_g": _row(ln1_g[layer]), "ln1_b": _row(ln1_b[layer]),
            "w_gate": _mx(w_gate[layer]), "w_up": _mx(w_up[layer]), "w_down": _mx(w_down[layer]),
            "ln2_g": _row(ln2_g[layer]), "ln2_b": _row(ln2_b[layer]),
        }
        y_prompt = _encoder_layer(y_prompt, p)
        y_sample = _encoder_layer(y_sample, p)
    return (y_prompt, y_sample)
```

```python
import functools

import jax
import jax.numpy as jnp
from jax import lax
from jax.experimental import pallas as pl
from jax.experimental.pallas import tpu as pltpu

F32 = jnp.float32
MXU_DTYPE = jnp.bfloat16

D_MODEL = 1024
S5_WIDTH = 512
S5_GROUP = 16
S5_GROUPS = S5_WIDTH // S5_GROUP
S5_STATE = 64
RWKV_WIDTH = 512
RWKV_HEAD = 64
DECAY_RANK = 64
ICLR_RANK = 64
GATE_RANK = 128
RWKV_IN_W = 3 * RWKV_WIDTH + 2 * DECAY_RANK + 2 * ICLR_RANK + GATE_RANK
D_FF = 2816
DEPTH = 1
ALPHA = (2.0 * DEPTH) ** 0.25
LN_EPS = 1e-5
GN_EPS = 64e-5
NORM_EPS = 1e-12

S5_CHUNK = 16
S5_CW = S5_CHUNK * S5_GROUP
S5_SLOT = 128
S5_LANES = 128
S5_SEQ_TILE = 8
RWKV_CHUNK = 64
PAIR = 2 * RWKV_HEAD
N_PAIRS = RWKV_WIDTH // PAIR
TOKEN_TILE = 512
VMEM_LIMIT = 56 * 1024 * 1024


def _mx(a):
    return a.astype(MXU_DTYPE)


def _dot(a, b):
    return jnp.dot(_mx(a), _mx(b), preferred_element_type=F32)


def _dot_nt(a, b):
    return lax.dot_general(_mx(a), _mx(b), (((1,), (1,)), ((), ())), preferred_element_type=F32)


def _split2(x):
    hi = x.astype(MXU_DTYPE)
    return hi, (x - hi.astype(F32)).astype(MXU_DTYPE)


def _dot_exact_rhs(x, m):
    hi, lo = _split2(x)
    return jnp.dot(hi, m, preferred_element_type=F32) + jnp.dot(lo, m, preferred_element_type=F32)


def _dot_exact_lhs(m, x):
    hi, lo = _split2(x)
    return jnp.dot(m, hi, preferred_element_type=F32) + jnp.dot(m, lo, preferred_element_type=F32)


def _layer_norm(v, g, b):
    mu = jnp.mean(v, axis=-1, keepdims=True)
    d = v - mu
    var = jnp.mean(d * d, axis=-1, keepdims=True)
    return d * lax.rsqrt(var + LN_EPS) * g + b


def _gelu_tanh(v):
    return 0.5 * v * (1.0 + jnp.tanh(0.7978845608028654 * (v + 0.044715 * (v * v * v))))


def _const_spec(shape):
    nd = len(shape)
    return pl.BlockSpec(shape, lambda *_: (0,) * nd)


def _proj_kernel(x_ref, xp_ref, xn_ref, wu_ref, wz_ref, mu_ref, u_ref, z_ref, *, seqlen):
    tm = x_ref.shape[0]
    x = x_ref[...]
    u_ref[...] = jnp.dot(_mx(x), wu_ref[...], preferred_element_type=F32)
    x_ext = jnp.concatenate([xp_ref[...], x, xn_ref[...]], axis=0)
    z_ext = jnp.dot(_mx(x_ext), wz_ref[...], preferred_element_type=F32)
    first = pl.program_id(0) * tm
    row = lax.broadcasted_iota(jnp.int32, (tm, 1), 0)
    has_prev = (row > 0) | (first % seqlen != 0)
    has_next = (row < tm - 1) | ((first + tm) % seqlen != 0)
    z = z_ext[8:8 + tm]
    prev = jnp.where(has_prev, pltpu.roll(z_ext, 1, 0)[8:8 + tm], 0.0)
    nxt = jnp.where(has_next, pltpu.roll(z_ext, tm + 15, 0)[8:8 + tm], 0.0)
    z_ref[...] = z + (0.5 * (prev + nxt) - z) * mu_ref[...]


def _proj(x2, wu, wz, mu, seqlen):
    n = x2.shape[0]
    tm = min(TOKEN_TILE, seqlen)
    blocks8 = tm // 8
    last8 = n // 8 - 1
    return pl.pallas_call(
        functools.partial(_proj_kernel, seqlen=seqlen),
        name="proj",
        grid=(n // tm,),
        in_specs=[pl.BlockSpec((tm, D_MODEL), lambda i: (i, 0)),
                  pl.BlockSpec((8, D_MODEL), lambda i: (jnp.maximum(i * blocks8 - 1, 0), 0)),
                  pl.BlockSpec((8, D_MODEL), lambda i: (jnp.minimum((i + 1) * blocks8, last8), 0)),
                  _const_spec(wu.shape), _const_spec(wz.shape), _const_spec(mu.shape)],
        out_specs=[pl.BlockSpec((tm, S5_WIDTH), lambda i: (i, 0)),
                   pl.BlockSpec((tm, RWKV_IN_W), lambda i: (i, 0))],
        out_shape=[jax.ShapeDtypeStruct((n, S5_WIDTH), F32),
                   jax.ShapeDtypeStruct((n, RWKV_IN_W), F32)],
        compiler_params=pltpu.CompilerParams(dimension_semantics=("arbitrary",),
                                             vmem_limit_bytes=VMEM_LIMIT),
    )(x2, x2, x2, wu, wz, mu)


def _s5_chunk_matrices(lam_re, lam_im, log_dt, b_re, b_im, c_re, c_im, d_skip):
    hp = lax.Precision.HIGHEST
    t = S5_CHUNK
    g = S5_GROUPS
    dt = jnp.exp(log_dt.astype(F32))[..., None]
    lre = lam_re.astype(F32)
    lim = lam_im.astype(F32)
    zr = lre * dt
    zi = lim * dt
    n = jnp.arange(t + 1, dtype=F32)[:, None, None, None]
    mag = jnp.exp(n * zr)
    pr = mag * jnp.cos(n * zi)
    pi = mag * jnp.sin(n * zi)
    den = lre * lre + lim * lim
    lbr = pr[1] - 1.0
    lbi = pi[1]
    qr = (lbr * lre + lbi * lim) / den
    qi = (lbi * lre - lbr * lim) / den
    bre = b_re.astype(F32)
    bim = b_im.astype(F32)
    bbr = qr[..., None] * bre - qi[..., None] * bim
    bbi = qr[..., None] * bim + qi[..., None] * bre
    wr = pr[..., None] * bbr - pi[..., None] * bbi
    wi = pr[..., None] * bbi + pi[..., None] * bbr
    cre = c_re.astype(F32)
    cim = c_im.astype(F32)
    kern = (jnp.einsum('dghp,ndgpk->ndghk', cre, wr[:t], precision=hp)
            - jnp.einsum('dghp,ndgpk->ndghk', cim, wi[:t], precision=hp))
    jj = jnp.arange(t)[:, None]
    ii = jnp.arange(t)[None, :]
    kf = kern[:, 0][jnp.clip(ii - jj, 0, t - 1)]
    kb = kern[:, 1][jnp.clip(jj - ii, 0, t - 1)]
    kf = jnp.where((ii >= jj)[:, :, None, None, None], kf, 0.0)
    kb = jnp.where((jj >= ii)[:, :, None, None, None], kb, 0.0)
    t_mat = jnp.transpose(kf + kb, (2, 0, 4, 1, 3)).reshape(g, S5_CW, S5_CW)

    def rows_jh(a):
        a = jnp.transpose(a, (1, 0, 3, 2)).reshape(g, S5_CW, S5_STATE)
        return jnp.pad(a, ((0, 0), (0, 0), (0, S5_SLOT - S5_STATE)))

    m_in = jnp.concatenate([rows_jh(wr[:t, 0][::-1]), rows_jh(wi[:t, 0][::-1]),
                            rows_jh(wr[:t, 1]), rows_jh(wi[:t, 1])], axis=-1)

    def cols_ih(a):
        a = jnp.transpose(a, (1, 3, 0, 2)).reshape(g, S5_STATE, S5_CW)
        return jnp.pad(a, ((0, 0), (0, S5_SLOT - S5_STATE), (0, 0)))

    pf_r, pf_i = pr[1:, 0][:, :, None, :], pi[1:, 0][:, :, None, :]
    pb_r, pb_i = pr[1:, 1][::-1][:, :, None, :], pi[1:, 1][::-1][:, :, None, :]
    m_out = jnp.concatenate([
        cols_ih(cre[0][None] * pf_r - cim[0][None] * pf_i),
        cols_ih(-(cre[0][None] * pf_i + cim[0][None] * pf_r)),
        cols_ih(cre[1][None] * pb_r - cim[1][None] * pb_i),
        cols_ih(-(cre[1][None] * pb_i + cim[1][None] * pb_r))], axis=1)
    a_step = jnp.pad(jnp.stack([pr[t, 0], pi[t, 0], pr[t, 1], pi[t, 1]], axis=1),
                     ((0, 0), (0, 0), (0, S5_SLOT - S5_STATE)))
    return _mx(t_mat), _mx(m_in), _mx(m_out), a_step, d_skip.astype(F32).reshape(1, S5_WIDTH)


def _block_transpose8(xs):
    xs = list(xs)
    lane_block = lax.broadcasted_iota(jnp.int32, (1, S5_LANES), 1) // S5_GROUP
    for s in (4, 2, 1):
        keep = (lane_block & s) == 0
        for a in range(8):
            if a & s:
                continue
            lo, hi = xs[a], xs[a | s]
            xs[a] = jnp.where(keep, lo, pltpu.roll(hi, S5_GROUP * s, 1))
            xs[a | s] = jnp.where(keep, pltpu.roll(lo, S5_LANES - S5_GROUP * s, 1), hi)
    return xs


def _s5_kernel(u_ref, t_ref, min_ref, mout_ref, a_ref, d_ref, y_ref, q_ref, yg_ref, sin_ref, xs_ref, *, n_chunks):
    bt = S5_SEQ_TILE
    rows = bt * n_chunks
    n_groups = S5_LANES // S5_GROUP
    slot = S5_SLOT

    for v in range(2):
        pieces = [u_ref[pl.ds(8 * v + a, rows, stride=S5_CHUNK), :] for a in range(8)]
        for gi, q in enumerate(_block_transpose8(pieces)):
            q_ref[gi, :, v * S5_LANES:(v + 1) * S5_LANES] = _mx(q)

    zero = jnp.zeros((bt, slot), F32)
    for gi in range(n_groups):
        q = q_ref[gi]
        s_in = jnp.dot(q, min_ref[gi], preferred_element_type=F32)
        for c in range(4):
            sin_ref[c] = s_in[:, c * slot:(c + 1) * slot]
        a = a_ref[gi]
        af_re, af_im, ab_re, ab_im = a[0:1], a[1:2], a[2:3], a[3:4]

        def step(k, carry):
            f_re, f_im, b_re, b_im = carry
            rk = pl.ds(k, bt, stride=n_chunks)
            rr = pl.ds(n_chunks - 1 - k, bt, stride=n_chunks)
            sf_re = sin_ref[0, rk, :]
            sf_im = sin_ref[1, rk, :]
            sb_re = sin_ref[2, rr, :]
            sb_im = sin_ref[3, rr, :]
            xs_ref[0, rk, :] = f_re
            xs_ref[1, rk, :] = f_im
            xs_ref[2, rr, :] = b_re
            xs_ref[3, rr, :] = b_im
            return (af_re * f_re - af_im * f_im + sf_re, af_re * f_im + af_im * f_re + sf_im,
                    ab_re * b_re - ab_im * b_im + sb_re, ab_re * b_im + ab_im * b_re + sb_im)

        lax.fori_loop(0, n_chunks, step, (zero, zero, zero, zero), unroll=4)
        states = jnp.concatenate([_mx(xs_ref[c]) for c in range(4)], axis=1)
        yg_ref[gi] = (jnp.dot(q, t_ref[gi], preferred_element_type=F32)
                      + jnp.dot(states, mout_ref[gi], preferred_element_type=F32))

    d = d_ref[...]
    for v in range(2):
        halves = [yg_ref[gi, :, v * S5_LANES:(v + 1) * S5_LANES] for gi in range(n_groups)]
        for a, y in enumerate(_block_transpose8(halves)):
            token_rows = pl.ds(8 * v + a, rows, stride=S5_CHUNK)
            y_ref[token_rows, :] = y + d * u_ref[token_rows, :]


def _s5_branch(u2, mats, seqlen):
    t_mat, m_in, m_out, a_step, d_row = mats
    n = u2.shape[0]
    n_chunks = seqlen // S5_CHUNK
    n_groups = S5_LANES // S5_GROUP
    tile_rows = S5_SEQ_TILE * seqlen
    pairs = S5_SEQ_TILE * n_chunks
    once = pl.Buffered(1)
    return pl.pallas_call(
        functools.partial(_s5_kernel, n_chunks=n_chunks),
        name="s5_scan",
        grid=(S5_WIDTH // S5_LANES, n // tile_rows),
        in_specs=[pl.BlockSpec((tile_rows, S5_LANES), lambda s, b: (b, s)),
                  pl.BlockSpec((n_groups, S5_CW, S5_CW), lambda s, b: (s, 0, 0), pipeline_mode=once),
                  pl.BlockSpec((n_groups, S5_CW, 4 * S5_SLOT), lambda s, b: (s, 0, 0), pipeline_mode=once),
                  pl.BlockSpec((n_groups, 4 * S5_SLOT, S5_CW), lambda s, b: (s, 0, 0), pipeline_mode=once),
                  pl.BlockSpec((n_groups, 4, S5_SLOT), lambda s, b: (s, 0, 0)),
                  pl.BlockSpec((1, S5_LANES), lambda s, b: (0, s))],
        out_specs=pl.BlockSpec((tile_rows, S5_LANES), lambda s, b: (b, s), pipeline_mode=once),
        out_shape=jax.ShapeDtypeStruct((n, S5_WIDTH), F32),
        scratch_shapes=[pltpu.VMEM((n_groups, pairs, S5_CW), MXU_DTYPE),
                        pltpu.VMEM((n_groups, pairs, S5_CW), F32),
                        pltpu.VMEM((4, pairs, S5_SLOT), F32),
                        pltpu.VMEM((4, pairs, S5_SLOT), F32)],
        compiler_params=pltpu.CompilerParams(dimension_semantics=("arbitrary", "arbitrary"),
                                             vmem_limit_bytes=VMEM_LIMIT),
    )(u2, t_mat, m_in, m_out, a_step, d_row)


def _rwkv_kernel(z_ref, w0_ref, w2_ref, a0_ref, a2_ref, g2_ref, kk_ref, ka_ref, rk_ref,
                 gng_ref, gnb_ref, o_ref, s_ref, y_ref, bon_ref, pa_ref, pg_ref, *, seqlen):
    t = RWKV_CHUNK
    n_chunks = seqlen // t
    w = RWKV_WIDTH

    tri_r = lax.broadcasted_iota(jnp.int32, (t, t), 0)
    tri_c = lax.broadcasted_iota(jnp.int32, (t, t), 1)
    seg_r = lax.broadcasted_iota(jnp.int32, (w, w), 0) // RWKV_HEAD
    seg_c = lax.broadcasted_iota(jnp.int32, (w, w), 1) // RWKV_HEAD
    seg_ones = jnp.where(seg_r == seg_c, 1.0, 0.0).astype(MXU_DTYPE)
    pr = lax.broadcasted_iota(jnp.int32, (PAIR, PAIR), 0)
    pc = lax.broadcasted_iota(jnp.int32, (PAIR, PAIR), 1)
    same_head = (pr // t) == (pc // t)
    eye = jnp.where(pr == pc, 1.0, 0.0).astype(F32)
    first_head = lax.broadcasted_iota(jnp.int32, (t, PAIR), 1) < RWKV_HEAD
    cum_ops = (jnp.where(tri_r >= tri_c, 1.0, 0.0).astype(MXU_DTYPE),
               jnp.where(tri_r <= tri_c, 1.0, 0.0).astype(MXU_DTYPE))
    strict = (same_head & (pr > pc), same_head & (pr < pc))
    incl = (same_head & (pr >= pc), same_head & (pr <= pc))

    k_k = kk_ref[...]
    k_a = ka_ref[...]
    r_k = rk_ref[...]

    s_ref[...] = jnp.zeros_like(s_ref)
    y_ref[...] = jnp.zeros_like(y_ref)
    bon_ref[...] = jnp.zeros_like(bon_ref)

    def stack(a, p):
        ap = a[:, p * PAIR:(p + 1) * PAIR]
        return jnp.concatenate([jnp.where(first_head, ap, 0.0), jnp.where(first_head, 0.0, ap)], axis=0)

    names = ("a", "r", "b", "k", "v", "be", "ke")

    def prep_gen(ci_f, ci_b, valid):
        for d, ci in ((0, ci_f), (1, ci_b)):
            rows = pl.ds(pl.multiple_of(ci * t, t), t)
            zs = z_ref[0, rows, :]
            r = zs[:, 0:w]
            k = zs[:, w:2 * w]
            v = zs[:, 2 * w:3 * w]
            w_low = zs[:, 3 * w:3 * w + 2 * DECAY_RANK]
            a_low = zs[:, 3 * w + 2 * DECAY_RANK:3 * w + 2 * DECAY_RANK + 2 * ICLR_RANK]
            if d == 0:
                o_ref[0, rows, :] = _dot(jax.nn.sigmoid(zs[:, 3 * w + 2 * DECAY_RANK + 2 * ICLR_RANK:]),
                                         g2_ref[...])
            w_log = w0_ref[d:d + 1, :] + _dot(jnp.tanh(w_low), w2_ref[d])
            yield
            w_log = -jnp.logaddexp(-w_log, 0.0) - 0.5
            logw = -jnp.exp(w_log)
            iclr = jax.nn.sigmoid(a0_ref[d:d + 1, :] + _dot(a_low, a2_ref[d]))
            yield
            kk = k * k_k
            nrm = jnp.sqrt(_dot(kk * kk, seg_ones))
            kk = kk / jnp.maximum(nrm, NORM_EPS)
            yield
            k_dir = k * (1.0 + (iclr - 1.0) * k_a)
            bon_ref[rows, :] += _dot_exact_rhs(r * k_dir * r_k, seg_ones) * (v * valid)
            yield
            cum = _dot_exact_lhs(cum_ops[d], logw)
            last = cum[t - 1:t, :] if d == 0 else cum[0:1, :]
            yield
            e_neg = jnp.exp(-cum)
            b_dir = kk * iclr
            e_end = jnp.exp(last - cum)
            vals = (-kk * jnp.exp(cum - logw), r * jnp.exp(cum), b_dir * e_neg, k_dir * e_neg, v,
                    b_dir * e_end, k_dir * e_end)
            yield
            for j, val in enumerate(vals):
                pa_ref[d, j] = val
            pg_ref[d, 0:1, :] = jnp.exp(last)
            yield

    chains = [(d, p) for d in range(2) for p in range(N_PAIRS)]

    def chain_gen(i):
        rows = (pl.ds(pl.multiple_of(i * t, t), t), pl.ds(pl.multiple_of((n_chunks - 1 - i) * t, t), t))
        ops = [{q: pa_ref[d, j] for j, q in enumerate(names)} for d in range(2)]
        g_end = [pg_ref[d, 0:1, :] for d in range(2)]
        st = {c: {q: _mx(stack(ops[c[0]][q], c[1])) for q in names} for c in chains}
        ar = {c: jnp.concatenate([st[c]["a"], st[c]["r"]], axis=0) for c in chains}
        sc = {c: _dot_nt(ar[c], jnp.concatenate([st[c]["b"], st[c]["k"]], axis=0)) for c in chains}
        yield
        l_ab = {c: jnp.where(strict[c[0]], sc[c][0:PAIR, 0:PAIR], 0.0) for c in chains}
        mix = {c: jnp.concatenate([jnp.where(strict[c[0]], sc[c][0:PAIR, PAIR:2 * PAIR], 0.0),
                                   jnp.where(incl[c[0]], sc[c][PAIR:2 * PAIR, PAIR:2 * PAIR], 0.0)], axis=0)
               for c in chains}
        s_old = {c: s_ref[c[0], c[1]] for c in chains}
        x = {c: _dot(jnp.concatenate([ar[c], _mx(mix[c])], axis=1),
                     jnp.concatenate([_mx(s_old[c].T), st[c]["v"]], axis=0)) for c in chains}
        yield
        m = {c: _dot(l_ab[c], l_ab[c]) for c in chains}
        inv = {c: eye + l_ab[c] for c in chains}
        yield
        for _ in range(4):
            mp = {c: _dot(m[c], jnp.concatenate([m[c], inv[c]], axis=1)) for c in chains}
            m = {c: mp[c][:, 0:PAIR] for c in chains}
            inv = {c: inv[c] + mp[c][:, PAIR:2 * PAIR] for c in chains}
            yield
        inv = {c: inv[c] + _dot(m[c], inv[c]) for c in chains}
        yield
        u = {c: _dot(inv[c], x[c][0:PAIR]) for c in chains}
        yield
        m_rb = {c: jnp.where(incl[c[0]], sc[c][PAIR:2 * PAIR, 0:PAIR], 0.0) for c in chains}
        o = {c: x[c][PAIR:2 * PAIR] + _dot(m_rb[c], u[c]) for c in chains}
        yield
        for c in chains:
            d, p = c
            lanes = slice(p * PAIR, (p + 1) * PAIR)
            y_ref[rows[d], lanes] += o[c][0:t] + o[c][t:2 * t]
            uv_t = jnp.concatenate([u[c].T, st[c]["v"].astype(F32).T], axis=1)
            s_ref[d, p] = (s_old[c] * g_end[d][:, lanes]
                           + _dot(uv_t, jnp.concatenate([st[c]["be"], st[c]["ke"]], axis=0)))
        yield

    def interleave(*gens):
        gens = list(gens)
        while gens:
            for g in list(gens):
                if next(g, gens) is gens:
                    gens.remove(g)

    def body(i, carry):
        nxt = jnp.minimum(i + 1, n_chunks - 1)
        valid = jnp.where(i + 1 < n_chunks, 1.0, 0.0)
        interleave(chain_gen(i), prep_gen(nxt, n_chunks - 1 - nxt, valid))
        return carry

    interleave(prep_gen(0, n_chunks - 1, 1.0))
    lax.fori_loop(0, n_chunks, body, 0)

    blk = min(seqlen, 256)
    gn_g = gng_ref[...]
    gn_b = gnb_ref[...]
    seg_mean = (seg_ones.astype(F32) * (1.0 / RWKV_HEAD)).astype(MXU_DTYPE)

    def norm_body(i, carry):
        rows = pl.ds(pl.multiple_of(i * blk, blk), blk)
        y = y_ref[rows, :]
        dlt = y - _dot_exact_rhs(y, seg_mean)
        var = _dot(dlt * dlt, seg_mean)
        yn = dlt * lax.rsqrt(var + GN_EPS) * gn_g + gn_b
        o_ref[0, rows, :] = (yn + bon_ref[rows, :]) * o_ref[0, rows, :]
        return carry

    lax.fori_loop(0, seqlen // blk, norm_body, 0)


def _rwkv_branch(z, p):
    bsz, seqlen, _ = z.shape
    consts = [p["w0"], p["w2"], p["a0"], p["a2"], p["g2"], p["k_k"], p["k_a"], p["r_k"],
              p["gn_g"], p["gn_b"]]
    return pl.pallas_call(
        functools.partial(_rwkv_kernel, seqlen=seqlen),
        name="rwkv",
        grid=(bsz,),
        in_specs=[pl.BlockSpec((1, seqlen, RWKV_IN_W), lambda i: (i, 0, 0))]
                 + [_const_spec(c.shape) for c in consts],
        out_specs=pl.BlockSpec((1, seqlen, RWKV_WIDTH), lambda i: (i, 0, 0)),
        out_shape=jax.ShapeDtypeStruct((bsz, seqlen, RWKV_WIDTH), F32),
        scratch_shapes=[pltpu.VMEM((2, N_PAIRS, PAIR, PAIR), F32),
                        pltpu.VMEM((seqlen, RWKV_WIDTH), F32),
                        pltpu.VMEM((seqlen, RWKV_WIDTH), F32),
                        pltpu.VMEM((2, 7, RWKV_CHUNK, RWKV_WIDTH), F32),
                        pltpu.VMEM((2, 8, RWKV_WIDTH), F32)],
        compiler_params=pltpu.CompilerParams(dimension_semantics=("arbitrary",),
                                             vmem_limit_bytes=VMEM_LIMIT),
    )(z, *consts)


def _merge_kernel(x_ref, ys_ref, yr_ref, wg_ref, wglu_ref, bglu_ref, wpa_ref, wpb_ref, wout_ref,
                  g_ref, b_ref, h_ref):
    x = x_ref[...]
    xb = _mx(x)
    act = _gelu_tanh(ys_ref[...])
    out_a = act * jax.nn.sigmoid(_dot(act, wglu_ref[...]) + bglu_ref[...])
    gate_a = jax.nn.sigmoid(jnp.dot(xb, wg_ref[:, 0:D_MODEL], preferred_element_type=F32))
    merged = gate_a * _dot(out_a, wpa_ref[...])
    gate_b = jax.nn.sigmoid(jnp.dot(xb, wg_ref[:, D_MODEL:2 * D_MODEL], preferred_element_type=F32))
    merged = merged + gate_b * _dot(yr_ref[...], wpb_ref[...])
    h_ref[...] = _layer_norm(ALPHA * x + _dot(merged, wout_ref[...]), g_ref[...], b_ref[...])


def _merge(x2, ys, yr, p):
    n = x2.shape[0]
    tm = min(TOKEN_TILE, n)
    consts = [p["w_gates"], p["w_glu"], p["b_glu"], p["w_pa"], p["w_pb"], p["w_out"], p["ln1_g"], p["ln1_b"]]
    return pl.pallas_call(
        _merge_kernel,
        name="merge",
        grid=(n // tm,),
        in_specs=[pl.BlockSpec((tm, D_MODEL), lambda i: (i, 0)),
                  pl.BlockSpec((tm, S5_WIDTH), lambda i: (i, 0)),
                  pl.BlockSpec((tm, RWKV_WIDTH), lambda i: (i, 0))]
                 + [_const_spec(c.shape) for c in consts],
        out_specs=pl.BlockSpec((tm, D_MODEL), lambda i: (i, 0)),
        out_shape=jax.ShapeDtypeStruct((n, D_MODEL), F32),
        compiler_params=pltpu.CompilerParams(dimension_semantics=("arbitrary",),
                                             vmem_limit_bytes=VMEM_LIMIT),
    )(x2, ys, yr, *consts)


def _ffn_kernel(h_ref, wg_ref, wu_ref, wd_ref, g_ref, b_ref, y_ref):
    h = h_ref[...]
    hb = _mx(h)
    half = D_FF // 2
    acc = ALPHA * h
    for c in range(2):
        cols = slice(c * half, (c + 1) * half)
        gate = jnp.dot(hb, wg_ref[:, cols], preferred_element_type=F32)
        up = jnp.dot(hb, wu_ref[:, cols], preferred_element_type=F32)
        acc = acc + _dot(gate * jax.nn.sigmoid(gate) * up, wd_ref[cols, :])
    y_ref[...] = _layer_norm(acc, g_ref[...], b_ref[...])


def _ffn(h2, p):
    n = h2.shape[0]
    tm = min(TOKEN_TILE, n)
    consts = [p["w_gate"], p["w_up"], p["w_down"], p["ln2_g"], p["ln2_b"]]
    return pl.pallas_call(
        _ffn_kernel,
        name="ffn",
        grid=(n // tm,),
        in_specs=[pl.BlockSpec((tm, D_MODEL), lambda i: (i, 0))] + [_const_spec(c.shape) for c in consts],
        out_specs=pl.BlockSpec((tm, D_MODEL), lambda i: (i, 0)),
        out_shape=jax.ShapeDtypeStruct((n, D_MODEL), F32),
        compiler_params=pltpu.CompilerParams(dimension_semantics=("arbitrary",),
                                             vmem_limit_bytes=VMEM_LIMIT),
    )(h2, *consts)


def _pad_rank(w2):
    z = jnp.zeros_like(w2[0])
    return jnp.stack([jnp.concatenate([w2[0], z], axis=0), jnp.concatenate([z, w2[1]], axis=0)])


def _row(v):
    return v.astype(F32).reshape(1, -1)


def _encoder_layer(x, p):
    bsz, seqlen, _ = x.shape
    x2 = x.reshape(bsz * seqlen, D_MODEL)
    u, z = _proj(x2, p["w_u"], p["w_z"], p["mu"], seqlen)
    ys = _s5_branch(u, p["s5_mats"], seqlen)
    yr = _rwkv_branch(z.reshape(bsz, seqlen, RWKV_IN_W), p).reshape(bsz * seqlen, RWKV_WIDTH)
    h = _merge(x2, ys, yr, p)
    return _ffn(h, p).reshape(bsz, seqlen, D_MODEL)


def kernel(x_prompt, x_sample, w_in, s5_lam_re, s5_lam_im, s5_log_dt, s5_b_re, s5_b_im, s5_c_re, s5_c_im,
           s5_d, s5_w_glu, s5_b_glu, rwkv_mu, rwkv_w0, rwkv_w2, rwkv_a0, rwkv_a2, rwkv_g2, rwkv_k_k,
           rwkv_k_a, rwkv_r_k, rwkv_gn_g, rwkv_gn_b, w_pa, w_pb, w_out, ln1_g, ln1_b, w_gate, w_up,
           w_down, ln2_g, ln2_b):
    y_prompt, y_sample = x_prompt, x_sample
    for layer in range(w_in.shape[0]):
        wl = w_in[layer]
        gate_start = S5_WIDTH + RWKV_IN_W
        p = {
            "w_u": _mx(wl[:, :S5_WIDTH]),
            "w_z": _mx(wl[:, S5_WIDTH:gate_start]),
            "w_gates": _mx(wl[:, gate_start:]),
            "s5_mats": _s5_chunk_matrices(s5_lam_re[layer], s5_lam_im[layer], s5_log_dt[layer],
                                          s5_b_re[layer], s5_b_im[layer], s5_c_re[layer], s5_c_im[layer],
                                          s5_d[layer]),
            "w_glu": _mx(s5_w_glu[layer]), "b_glu": _row(s5_b_glu[layer]),
            "mu": _row(rwkv_mu[layer]), "w0": rwkv_w0[layer].astype(F32),
            "w2": _mx(_pad_rank(rwkv_w2[layer])), "a0": rwkv_a0[layer].astype(F32),
            "a2": _mx(_pad_rank(rwkv_a2[layer])), "g2": _mx(rwkv_g2[layer]),
            "k_k": _row(rwkv_k_k[layer]), "k_a": _row(rwkv_k_a[layer]), "r_k": _row(rwkv_r_k[layer]),
            "gn_g": _row(rwkv_gn_g[layer]), "gn_b": _row(rwkv_gn_b[layer]),
            "w_pa": _mx(w_pa[layer]), "w_pb": _mx(w_pb[layer]), "w_out": _mx(w_out[layer]),
            "ln1_g": _row(ln1_g[layer]), "ln1_b": _row(ln1_b[layer]),
            "w_gate": _mx(w_gate[layer]), "w_up": _mx(w_up[layer]), "w_down": _mx(w_down[layer]),
            "ln2_g": _row(ln2_g[layer]), "ln2_b": _row(ln2_b[layer]),
        }
        y_prompt = _encoder_layer(y_prompt, p)
        y_sample = _encoder_layer(y_sample, p)
    return (y_prompt, y_sample)
```

```python
import functools

import jax
import jax.numpy as jnp
from jax import lax
from jax.experimental import pallas as pl
from jax.experimental.pallas import tpu as pltpu

F32 = jnp.float32
MXU_DTYPE = jnp.bfloat16

D_MODEL = 1024
S5_WIDTH = 512
S5_GROUP = 16
S5_GROUPS = S5_WIDTH // S5_GROUP
S5_STATE = 64
RWKV_WIDTH = 512
RWKV_HEAD = 64
DECAY_RANK = 64
ICLR_RANK = 64
GATE_RANK = 128
RWKV_IN_W = 3 * RWKV_WIDTH + 2 * DECAY_RANK + 2 * ICLR_RANK + GATE_RANK
D_FF = 2816
DEPTH = 1
ALPHA = (2.0 * DEPTH) ** 0.25
LN_EPS = 1e-5
GN_EPS = 64e-5
NORM_EPS = 1e-12

S5_CHUNK = 16
S5_CW = S5_CHUNK * S5_GROUP
S5_SLOT = 128
S5_LANES = 128
S5_SEQ_TILE = 8
RWKV_CHUNK = 64
PAIR = 2 * RWKV_HEAD
N_PAIRS = RWKV_WIDTH // PAIR
RWKV_SEQ_TILE = 2
TOKEN_TILE = 512
VMEM_LIMIT = 56 * 1024 * 1024


def _mx(a):
    return a.astype(MXU_DTYPE)


def _dot(a, b):
    return jnp.dot(_mx(a), _mx(b), preferred_element_type=F32)


def _dot_nt(a, b):
    return lax.dot_general(_mx(a), _mx(b), (((1,), (1,)), ((), ())), preferred_element_type=F32)


def _split2(x):
    hi = x.astype(MXU_DTYPE)
    return hi, (x - hi.astype(F32)).astype(MXU_DTYPE)


def _dot_exact_rhs(x, m):
    hi, lo = _split2(x)
    return jnp.dot(hi, m, preferred_element_type=F32) + jnp.dot(lo, m, preferred_element_type=F32)


def _dot_exact_lhs(m, x):
    hi, lo = _split2(x)
    return jnp.dot(m, hi, preferred_element_type=F32) + jnp.dot(m, lo, preferred_element_type=F32)


def _layer_norm(v, g, b):
    mu = jnp.mean(v, axis=-1, keepdims=True)
    d = v - mu
    var = jnp.mean(d * d, axis=-1, keepdims=True)
    return d * lax.rsqrt(var + LN_EPS) * g + b


def _gelu_tanh(v):
    return 0.5 * v * (1.0 + jnp.tanh(0.7978845608028654 * (v + 0.044715 * (v * v * v))))


def _const_spec(shape):
    nd = len(shape)
    return pl.BlockSpec(shape, lambda *_: (0,) * nd)


def _proj_kernel(x_ref, xp_ref, xn_ref, wu_ref, wz_ref, mu_ref, u_ref, z_ref, *, seqlen):
    tm = x_ref.shape[0]
    x = x_ref[...]
    u_ref[...] = jnp.dot(_mx(x), wu_ref[...], preferred_element_type=F32)
    x_ext = jnp.concatenate([xp_ref[...], x, xn_ref[...]], axis=0)
    z_ext = jnp.dot(_mx(x_ext), wz_ref[...], preferred_element_type=F32)
    first = pl.program_id(0) * tm
    row = lax.broadcasted_iota(jnp.int32, (tm, 1), 0)
    has_prev = (row > 0) | (first % seqlen != 0)
    has_next = (row < tm - 1) | ((first + tm) % seqlen != 0)
    z = z_ext[8:8 + tm]
    prev = jnp.where(has_prev, pltpu.roll(z_ext, 1, 0)[8:8 + tm], 0.0)
    nxt = jnp.where(has_next, pltpu.roll(z_ext, tm + 15, 0)[8:8 + tm], 0.0)
    z_ref[...] = (z + (0.5 * (prev + nxt) - z) * mu_ref[...]).astype(z_ref.dtype)


def _proj(x2, wu, wz, mu, seqlen):
    n = x2.shape[0]
    tm = min(TOKEN_TILE, seqlen)
    blocks8 = tm // 8
    last8 = n // 8 - 1
    return pl.pallas_call(
        functools.partial(_proj_kernel, seqlen=seqlen),
        name="proj",
        grid=(n // tm,),
        in_specs=[pl.BlockSpec((tm, D_MODEL), lambda i: (i, 0)),
                  pl.BlockSpec((8, D_MODEL), lambda i: (jnp.maximum(i * blocks8 - 1, 0), 0)),
                  pl.BlockSpec((8, D_MODEL), lambda i: (jnp.minimum((i + 1) * blocks8, last8), 0)),
                  _const_spec(wu.shape), _const_spec(wz.shape), _const_spec(mu.shape)],
        out_specs=[pl.BlockSpec((tm, S5_WIDTH), lambda i: (i, 0)),
                   pl.BlockSpec((tm, RWKV_IN_W), lambda i: (i, 0))],
        out_shape=[jax.ShapeDtypeStruct((n, S5_WIDTH), F32),
                   jax.ShapeDtypeStruct((n, RWKV_IN_W), MXU_DTYPE)],
        compiler_params=pltpu.CompilerParams(dimension_semantics=("arbitrary",),
                                             vmem_limit_bytes=VMEM_LIMIT),
    )(x2, x2, x2, wu, wz, mu)


def _s5_chunk_matrices(lam_re, lam_im, log_dt, b_re, b_im, c_re, c_im, d_skip):
    hp = lax.Precision.HIGHEST
    t = S5_CHUNK
    g = S5_GROUPS
    dt = jnp.exp(log_dt.astype(F32))[..., None]
    lre = lam_re.astype(F32)
    lim = lam_im.astype(F32)
    zr = lre * dt
    zi = lim * dt
    n = jnp.arange(t + 1, dtype=F32)[:, None, None, None]
    mag = jnp.exp(n * zr)
    pr = mag * jnp.cos(n * zi)
    pi = mag * jnp.sin(n * zi)
    den = lre * lre + lim * lim
    lbr = pr[1] - 1.0
    lbi = pi[1]
    qr = (lbr * lre + lbi * lim) / den
    qi = (lbi * lre - lbr * lim) / den
    bre = b_re.astype(F32)
    bim = b_im.astype(F32)
    bbr = qr[..., None] * bre - qi[..., None] * bim
    bbi = qr[..., None] * bim + qi[..., None] * bre
    wr = pr[..., None] * bbr - pi[..., None] * bbi
    wi = pr[..., None] * bbi + pi[..., None] * bbr
    cre = c_re.astype(F32)
    cim = c_im.astype(F32)
    kern = (jnp.einsum('dghp,ndgpk->ndghk', cre, wr[:t], precision=hp)
            - jnp.einsum('dghp,ndgpk->ndghk', cim, wi[:t], precision=hp))
    jj = jnp.arange(t)[:, None]
    ii = jnp.arange(t)[None, :]
    kf = kern[:, 0][jnp.clip(ii - jj, 0, t - 1)]
    kb = kern[:, 1][jnp.clip(jj - ii, 0, t - 1)]
    kf = jnp.where((ii >= jj)[:, :, None, None, None], kf, 0.0)
    kb = jnp.where((jj >= ii)[:, :, None, None, None], kb, 0.0)
    t_mat = jnp.transpose(kf + kb, (2, 0, 4, 1, 3)).reshape(g, S5_CW, S5_CW)

    def rows_jh(a):
        a = jnp.transpose(a, (1, 0, 3, 2)).reshape(g, S5_CW, S5_STATE)
        return jnp.pad(a, ((0, 0), (0, 0), (0, S5_SLOT - S5_STATE)))

    m_in = jnp.concatenate([rows_jh(wr[:t, 0][::-1]), rows_jh(wi[:t, 0][::-1]),
                            rows_jh(wr[:t, 1]), rows_jh(wi[:t, 1])], axis=-1)

    def cols_ih(a):
        a = jnp.transpose(a, (1, 3, 0, 2)).reshape(g, S5_STATE, S5_CW)
        return jnp.pad(a, ((0, 0), (0, S5_SLOT - S5_STATE), (0, 0)))

    pf_r, pf_i = pr[1:, 0][:, :, None, :], pi[1:, 0][:, :, None, :]
    pb_r, pb_i = pr[1:, 1][::-1][:, :, None, :], pi[1:, 1][::-1][:, :, None, :]
    m_out = jnp.concatenate([
        cols_ih(cre[0][None] * pf_r - cim[0][None] * pf_i),
        cols_ih(-(cre[0][None] * pf_i + cim[0][None] * pf_r)),
        cols_ih(cre[1][None] * pb_r - cim[1][None] * pb_i),
        cols_ih(-(cre[1][None] * pb_i + cim[1][None] * pb_r))], axis=1)
    a_step = jnp.pad(jnp.stack([pr[t, 0], pi[t, 0], pr[t, 1], pi[t, 1]], axis=1),
                     ((0, 0), (0, 0), (0, S5_SLOT - S5_STATE)))

    def halves(x, slot_axis):
        x = x.reshape((g // 2, 2) + x.shape[1:])
        return x[:, 0], jnp.roll(x[:, 1], S5_STATE, axis=slot_axis)

    in_a, in_b = halves(m_in.reshape(g, S5_CW, 4, S5_SLOT), -1)
    m_in = jnp.concatenate([in_a, in_b], axis=1).reshape(g // 2, 2 * S5_CW, 4 * S5_SLOT)
    out_a, out_b = halves(m_out.reshape(g, 4, S5_SLOT, S5_CW), -2)
    m_out = jnp.concatenate([out_a, out_b], axis=-1).reshape(g // 2, 4 * S5_SLOT, 2 * S5_CW)
    step_a, step_b = halves(a_step, -1)
    return (_mx(t_mat), _mx(m_in), _mx(m_out), step_a + step_b, d_skip.astype(F32).reshape(1, S5_WIDTH))


def _block_transpose8(xs):
    xs = list(xs)
    lane_block = lax.broadcasted_iota(jnp.int32, (1, S5_LANES), 1) // S5_GROUP
    for s in (4, 2, 1):
        keep = (lane_block & s) == 0
        for a in range(8):
            if a & s:
                continue
            lo, hi = xs[a], xs[a | s]
            xs[a] = jnp.where(keep, lo, pltpu.roll(hi, S5_GROUP * s, 1))
            xs[a | s] = jnp.where(keep, pltpu.roll(lo, S5_LANES - S5_GROUP * s, 1), hi)
    return xs


def _s5_kernel(u_ref, t_ref, min_ref, mout_ref, a_ref, d_ref, y_ref, q_ref, yg_ref, sin_ref, xs_ref, *, n_chunks):
    bt = S5_SEQ_TILE
    rows = bt * n_chunks
    n_groups = S5_LANES // S5_GROUP
    slot = S5_SLOT

    for v in range(2):
        pieces = [u_ref[pl.ds(8 * v + a, rows, stride=S5_CHUNK), :] for a in range(8)]
        for gi, q in enumerate(_block_transpose8(pieces)):
            q_ref[gi, :, v * S5_LANES:(v + 1) * S5_LANES] = _mx(q)

    zero = jnp.zeros((bt, slot), F32)
    for gp in range(n_groups // 2):
        q = (q_ref[2 * gp], q_ref[2 * gp + 1])
        s_in = jnp.dot(jnp.concatenate(q, axis=1), min_ref[gp], preferred_element_type=F32)
        for c in range(4):
            sin_ref[c] = s_in[:, c * slot:(c + 1) * slot]
        a = a_ref[gp]
        af_re, af_im, ab_re, ab_im = a[0:1], a[1:2], a[2:3], a[3:4]

        def step(k, carry):
            f_re, f_im, b_re, b_im = carry
            rk = pl.ds(k, bt, stride=n_chunks)
            rr = pl.ds(n_chunks - 1 - k, bt, stride=n_chunks)
            sf_re = sin_ref[0, rk, :]
            sf_im = sin_ref[1, rk, :]
            sb_re = sin_ref[2, rr, :]
            sb_im = sin_ref[3, rr, :]
            xs_ref[0, rk, :] = f_re
            xs_ref[1, rk, :] = f_im
            xs_ref[2, rr, :] = b_re
            xs_ref[3, rr, :] = b_im
            return (af_re * f_re - af_im * f_im + sf_re, af_re * f_im + af_im * f_re + sf_im,
                    ab_re * b_re - ab_im * b_im + sb_re, ab_re * b_im + ab_im * b_re + sb_im)

        lax.fori_loop(0, n_chunks, step, (zero, zero, zero, zero), unroll=4)
        states = jnp.concatenate([_mx(xs_ref[c]) for c in range(4)], axis=1)
        from_state = jnp.dot(states, mout_ref[gp], preferred_element_type=F32)
        for e in range(2):
            yg_ref[2 * gp + e] = (jnp.dot(q[e], t_ref[2 * gp + e], preferred_element_type=F32)
                                  + from_state[:, e * S5_CW:(e + 1) * S5_CW])

    d = d_ref[...]
    for v in range(2):
        halves = [yg_ref[gi, :, v * S5_LANES:(v + 1) * S5_LANES] for gi in range(n_groups)]
        for a, y in enumerate(_block_transpose8(halves)):
            token_rows = pl.ds(8 * v + a, rows, stride=S5_CHUNK)
            y_ref[token_rows, :] = y + d * u_ref[token_rows, :]


def _s5_branch(u2, mats, seqlen):
    t_mat, m_in, m_out, a_step, d_row = mats
    n = u2.shape[0]
    n_chunks = seqlen // S5_CHUNK
    n_groups = S5_LANES // S5_GROUP
    tile_rows = S5_SEQ_TILE * seqlen
    pairs = S5_SEQ_TILE * n_chunks
    once = pl.Buffered(1)
    return pl.pallas_call(
        functools.partial(_s5_kernel, n_chunks=n_chunks),
        name="s5_scan",
        grid=(S5_WIDTH // S5_LANES, n // tile_rows),
        in_specs=[pl.BlockSpec((tile_rows, S5_LANES), lambda s, b: (b, s)),
                  pl.BlockSpec((n_groups, S5_CW, S5_CW), lambda s, b: (s, 0, 0), pipeline_mode=once),
                  pl.BlockSpec((n_groups // 2, 2 * S5_CW, 4 * S5_SLOT), lambda s, b: (s, 0, 0), pipeline_mode=once),
                  pl.BlockSpec((n_groups // 2, 4 * S5_SLOT, 2 * S5_CW), lambda s, b: (s, 0, 0), pipeline_mode=once),
                  pl.BlockSpec((n_groups // 2, 4, S5_SLOT), lambda s, b: (s, 0, 0)),
                  pl.BlockSpec((1, S5_LANES), lambda s, b: (0, s))],
        out_specs=pl.BlockSpec((tile_rows, S5_LANES), lambda s, b: (b, s), pipeline_mode=once),
        out_shape=jax.ShapeDtypeStruct((n, S5_WIDTH), F32),
        scratch_shapes=[pltpu.VMEM((n_groups, pairs, S5_CW), MXU_DTYPE),
                        pltpu.VMEM((n_groups, pairs, S5_CW), F32),
                        pltpu.VMEM((4, pairs, S5_SLOT), F32),
                        pltpu.VMEM((4, pairs, S5_SLOT), F32)],
        compiler_params=pltpu.CompilerParams(dimension_semantics=("arbitrary", "arbitrary"),
                                             vmem_limit_bytes=VMEM_LIMIT),
    )(u2, t_mat, m_in, m_out, a_step, d_row)


def _rwkv_kernel(z_ref, w0_ref, w2_ref, a0_ref, a2_ref, g2_ref, kk_ref, ka_ref, rk_ref,
                 gng_ref, gnb_ref, o_ref, s_ref, y_ref, pa_ref, pg_ref, *, seqlen):
    n_seq = z_ref.shape[0]
    t = RWKV_CHUNK
    n_chunks = seqlen // t
    w = RWKV_WIDTH

    tri_r = lax.broadcasted_iota(jnp.int32, (t, t), 0)
    tri_c = lax.broadcasted_iota(jnp.int32, (t, t), 1)
    pr = lax.broadcasted_iota(jnp.int32, (PAIR, PAIR), 0)
    pc = lax.broadcasted_iota(jnp.int32, (PAIR, PAIR), 1)
    seg_ones = jnp.where(pr // RWKV_HEAD == pc // RWKV_HEAD, 1.0, 0.0).astype(MXU_DTYPE)

    def head_sums(a, op, dot):
        return jnp.concatenate([dot(a[:, p * PAIR:(p + 1) * PAIR], op) for p in range(N_PAIRS)], axis=1)

    same_head = (pr // t) == (pc // t)
    eye = jnp.where(pr == pc, 1.0, 0.0).astype(F32)
    first_head = lax.broadcasted_iota(jnp.int32, (t, PAIR), 1) < RWKV_HEAD
    cum_ops = (jnp.where(tri_r >= tri_c, 1.0, 0.0).astype(MXU_DTYPE),
               jnp.where(tri_r <= tri_c, 1.0, 0.0).astype(MXU_DTYPE))
    strict = (same_head & (pr > pc), same_head & (pr < pc))
    incl = (same_head & (pr >= pc), same_head & (pr <= pc))

    k_k = kk_ref[...]
    k_a = ka_ref[...]
    r_k = rk_ref[...]

    s_ref[...] = jnp.zeros_like(s_ref)
    y_ref[...] = jnp.zeros_like(y_ref)
    o_ref[...] = jnp.zeros_like(o_ref)

    def stack(a, p):
        ap = a[:, p * PAIR:(p + 1) * PAIR]
        return jnp.concatenate([jnp.where(first_head, ap, 0.0), jnp.where(first_head, 0.0, ap)], axis=0)

    names = ("a", "r", "b", "k", "v", "be", "ke")

    def prep_gen(ci_f, ci_b, valid):
        for q, d, ci in [(q, d, ci) for q in range(n_seq) for d, ci in ((0, ci_f), (1, ci_b))]:
            rows = pl.ds(pl.multiple_of(ci * t, t), t)
            zs = z_ref[q, rows, 0:3 * w + 2 * DECAY_RANK + 2 * ICLR_RANK].astype(F32)
            r = zs[:, 0:w]
            k = zs[:, w:2 * w]
            v = zs[:, 2 * w:3 * w]
            w_low = zs[:, 3 * w:3 * w + 2 * DECAY_RANK]
            a_low = zs[:, 3 * w + 2 * DECAY_RANK:3 * w + 2 * DECAY_RANK + 2 * ICLR_RANK]
            w_log = w0_ref[d:d + 1, :] + _dot(jnp.tanh(w_low), w2_ref[d])
            yield
            w_log = -jnp.logaddexp(-w_log, 0.0) - 0.5
            logw = -jnp.exp(w_log)
            iclr = jax.nn.sigmoid(a0_ref[d:d + 1, :] + _dot(a_low, a2_ref[d]))
            yield
            kk = k * k_k
            nrm = jnp.sqrt(head_sums(kk * kk, seg_ones, _dot))
            kk = kk / jnp.maximum(nrm, NORM_EPS)
            yield
            k_dir = k * (1.0 + (iclr - 1.0) * k_a)
            o_ref[q, rows, :] += head_sums(r * k_dir * r_k, seg_ones, _dot_exact_rhs) * (v * valid)
            yield
            cum = _dot_exact_lhs(cum_ops[d], logw)
            last = cum[t - 1:t, :] if d == 0 else cum[0:1, :]
            yield
            e_neg = jnp.exp(-cum)
            b_dir = kk * iclr
            e_end = jnp.exp(last - cum)
            vals = (-kk * jnp.exp(cum - logw), r * jnp.exp(cum), b_dir * e_neg, k_dir * e_neg, v,
                    b_dir * e_end, k_dir * e_end)
            yield
            for j, val in enumerate(vals):
                pa_ref[q, d, j] = val
            pg_ref[q, d, 0:1, :] = jnp.exp(last)
            yield

    chains = [(q, d, p) for q in range(n_seq) for d in range(2) for p in range(N_PAIRS)]

    def chain_gen(i):
        rows = (pl.ds(pl.multiple_of(i * t, t), t), pl.ds(pl.multiple_of((n_chunks - 1 - i) * t, t), t))
        st = {(q, d, p): {nm: _mx(stack(pa_ref[q, d, j], p)) for j, nm in enumerate(names)} for q, d, p in chains}
        g_end = {(q, d): pg_ref[q, d, 0:1, :] for q in range(n_seq) for d in range(2)}
        ar = {c: jnp.concatenate([st[c]["a"], st[c]["r"]], axis=0) for c in chains}
        sc = {c: _dot_nt(ar[c], jnp.concatenate([st[c]["b"], st[c]["k"]], axis=0)) for c in chains}
        yield
        l_ab = {c: jnp.where(strict[c[1]], sc[c][0:PAIR, 0:PAIR], 0.0) for c in chains}
        mix = {c: _mx(jnp.concatenate([jnp.where(strict[c[1]], sc[c][0:PAIR, PAIR:2 * PAIR], 0.0),
                                       jnp.where(incl[c[1]], sc[c][PAIR:2 * PAIR, PAIR:2 * PAIR], 0.0)], axis=0))
               for c in chains}
        m_rb = {c: _mx(jnp.where(incl[c[1]], sc[c][PAIR:2 * PAIR, 0:PAIR], 0.0)) for c in chains}
        s_old = {c: s_ref[c] for c in chains}
        x = {c: _dot(jnp.concatenate([ar[c], mix[c]], axis=1),
                     jnp.concatenate([_mx(s_old[c].T), st[c]["v"]], axis=0)) for c in chains}
        yield
        m = {c: _dot(l_ab[c], l_ab[c]) for c in chains}
        inv = {c: eye + l_ab[c] for c in chains}
        yield
        for _ in range(4):
            mp = {c: _dot(m[c], jnp.concatenate([m[c], inv[c]], axis=1)) for c in chains}
            m = {c: mp[c][:, 0:PAIR] for c in chains}
            inv = {c: inv[c] + mp[c][:, PAIR:2 * PAIR] for c in chains}
            yield
        inv = {c: inv[c] + _dot(m[c], inv[c]) for c in chains}
        yield
        u = {c: _dot(inv[c], x[c][0:PAIR]) for c in chains}
        yield
        o = {c: x[c][PAIR:2 * PAIR] + _dot(m_rb[c], u[c]) for c in chains}
        yield
        for c in chains:
            q, d, p = c
            lanes = slice(p * PAIR, (p + 1) * PAIR)
            y_ref[q, rows[d], lanes] += o[c][0:t] + o[c][t:2 * t]
            uv_t = jnp.concatenate([u[c].T, st[c]["v"].astype(F32).T], axis=1)
            s_ref[c] = (s_old[c] * g_end[q, d][:, lanes]
                        + _dot(uv_t, jnp.concatenate([st[c]["be"], st[c]["ke"]], axis=0)))
        yield

    def interleave(main, side, side_per_main):
        for _ in main:
            for _ in range(side_per_main):
                next(side, None)
        for _ in side:
            pass

    prep_stages = 7 * 2 * n_seq
    chain_stages = 11

    def body(i, carry):
        nxt = jnp.minimum(i + 1, n_chunks - 1)
        valid = jnp.where(i + 1 < n_chunks, 1.0, 0.0)
        interleave(chain_gen(i), prep_gen(nxt, n_chunks - 1 - nxt, valid), -(-prep_stages // chain_stages))
        return carry

    for _ in prep_gen(0, n_chunks - 1, 1.0):
        pass
    lax.fori_loop(0, n_chunks, body, 0)

    blk = min(seqlen, 256)
    gn_g = gng_ref[...]
    gn_b = gnb_ref[...]
    seg_mean = (seg_ones.astype(F32) * (1.0 / RWKV_HEAD)).astype(MXU_DTYPE)

    def norm_body(i, carry):
        rows = pl.ds(pl.multiple_of(i * blk, blk), blk)
        for q in range(n_seq):
            y = y_ref[q, rows, :]
            dlt = y - head_sums(y, seg_mean, _dot_exact_rhs)
            var = head_sums(dlt * dlt, seg_mean, _dot)
            yn = dlt * lax.rsqrt(var + GN_EPS) * gn_g + gn_b
            gate = _dot(jax.nn.sigmoid(z_ref[q, rows, RWKV_IN_W - GATE_RANK:].astype(F32)), g2_ref[...])
            o_ref[q, rows, :] = (yn + o_ref[q, rows, :]) * gate
        return carry

    lax.fori_loop(0, seqlen // blk, norm_body, 0)


def _rwkv_branch(z, p):
    bsz, seqlen, _ = z.shape
    consts = [p["w0"], p["w2"], p["a0"], p["a2"], p["g2"], p["k_k"], p["k_a"], p["r_k"],
              p["gn_g"], p["gn_b"]]
    ns = RWKV_SEQ_TILE
    return pl.pallas_call(
        functools.partial(_rwkv_kernel, seqlen=seqlen),
        name="rwkv",
        grid=(bsz // ns,),
        in_specs=[pl.BlockSpec((ns, seqlen, RWKV_IN_W), lambda i: (i, 0, 0), pipeline_mode=pl.Buffered(1))]
                 + [_const_spec(c.shape) for c in consts],
        out_specs=pl.BlockSpec((ns, seqlen, RWKV_WIDTH), lambda i: (i, 0, 0)),
        out_shape=jax.ShapeDtypeStruct((bsz, seqlen, RWKV_WIDTH), F32),
        scratch_shapes=[pltpu.VMEM((ns, 2, N_PAIRS, PAIR, PAIR), F32),
                        pltpu.VMEM((ns, seqlen, RWKV_WIDTH), F32),
                        pltpu.VMEM((ns, 2, 7, RWKV_CHUNK, RWKV_WIDTH), F32),
                        pltpu.VMEM((ns, 2, 8, RWKV_WIDTH), F32)],
        compiler_params=pltpu.CompilerParams(dimension_semantics=("arbitrary",),
                                             vmem_limit_bytes=VMEM_LIMIT),
    )(z, *consts)


def _merge_kernel(x_ref, ys_ref, yr_ref, wg_ref, wglu_ref, bglu_ref, wpa_ref, wpb_ref, wout_ref,
                  g_ref, b_ref, h_ref):
    x = x_ref[...]
    xb = _mx(x)
    act = _gelu_tanh(ys_ref[...])
    out_a = act * jax.nn.sigmoid(_dot(act, wglu_ref[...]) + bglu_ref[...])
    gate_a = jax.nn.sigmoid(jnp.dot(xb, wg_ref[:, 0:D_MODEL], preferred_element_type=F32))
    merged = gate_a * _dot(out_a, wpa_ref[...])
    gate_b = jax.nn.sigmoid(jnp.dot(xb, wg_ref[:, D_MODEL:2 * D_MODEL], preferred_element_type=F32))
    merged = merged + gate_b * _dot(yr_ref[...], wpb_ref[...])
    h_ref[...] = _layer_norm(ALPHA * x + _dot(merged, wout_ref[...]), g_ref[...], b_ref[...])


def _merge(x2, ys, yr, p):
    n = x2.shape[0]
    tm = min(TOKEN_TILE, n)
    consts = [p["w_gates"], p["w_glu"], p["b_glu"], p["w_pa"], p["w_pb"], p["w_out"], p["ln1_g"], p["ln1_b"]]
    return pl.pallas_call(
        _merge_kernel,
        name="merge",
        grid=(n // tm,),
        in_specs=[pl.BlockSpec((tm, D_MODEL), lambda i: (i, 0)),
                  pl.BlockSpec((tm, S5_WIDTH), lambda i: (i, 0)),
                  pl.BlockSpec((tm, RWKV_WIDTH), lambda i: (i, 0))]
                 + [_const_spec(c.shape) for c in consts],
        out_specs=pl.BlockSpec((tm, D_MODEL), lambda i: (i, 0)),
        out_shape=jax.ShapeDtypeStruct((n, D_MODEL), F32),
        compiler_params=pltpu.CompilerParams(dimension_semantics=("arbitrary",),
                                             vmem_limit_bytes=VMEM_LIMIT),
    )(x2, ys, yr, *consts)


def _ffn_kernel(h_ref, wg_ref, wu_ref, wd_ref, g_ref, b_ref, y_ref):
    h = h_ref[...]
    hb = _mx(h)
    half = D_FF // 2
    acc = ALPHA * h
    for c in range(2):
        cols = slice(c * half, (c + 1) * half)
        gate = jnp.dot(hb, wg_ref[:, cols], preferred_element_type=F32)
        up = jnp.dot(hb, wu_ref[:, cols], preferred_element_type=F32)
        acc = acc + _dot(gate * jax.nn.sigmoid(gate) * up, wd_ref[cols, :])
    y_ref[...] = _layer_norm(acc, g_ref[...], b_ref[...])


def _ffn(h2, p):
    n = h2.shape[0]
    tm = min(TOKEN_TILE, n)
    consts = [p["w_gate"], p["w_up"], p["w_down"], p["ln2_g"], p["ln2_b"]]
    return pl.pallas_call(
        _ffn_kernel,
        name="ffn",
        grid=(n // tm,),
        in_specs=[pl.BlockSpec((tm, D_MODEL), lambda i: (i, 0))] + [_const_spec(c.shape) for c in consts],
        out_specs=pl.BlockSpec((tm, D_MODEL), lambda i: (i, 0)),
        out_shape=jax.ShapeDtypeStruct((n, D_MODEL), F32),
        compiler_params=pltpu.CompilerParams(dimension_semantics=("arbitrary",),
                                             vmem_limit_bytes=VMEM_LIMIT),
    )(h2, *consts)


def _pad_rank(w2):
    z = jnp.zeros_like(w2[0])
    return jnp.stack([jnp.concatenate([w2[0], z], axis=0), jnp.concatenate([z, w2[1]], axis=0)])


def _row(v):
    return v.astype(F32).reshape(1, -1)


def _encoder_layer(x, p):
    bsz, seqlen, _ = x.shape
    x2 = x.reshape(bsz * seqlen, D_MODEL)
    u, z = _proj(x2, p["w_u"], p["w_z"], p["mu"], seqlen)
    ys = _s5_branch(u, p["s5_mats"], seqlen)
    yr = _rwkv_branch(z.reshape(bsz, seqlen, RWKV_IN_W), p).reshape(bsz * seqlen, RWKV_WIDTH)
    h = _merge(x2, ys, yr, p)
    return _ffn(h, p).reshape(bsz, seqlen, D_MODEL)


def kernel(x_prompt, x_sample, w_in, s5_lam_re, s5_lam_im, s5_log_dt, s5_b_re, s5_b_im, s5_c_re, s5_c_im,
           s5_d, s5_w_glu, s5_b_glu, rwkv_mu, rwkv_w0, rwkv_w2, rwkv_a0, rwkv_a2, rwkv_g2, rwkv_k_k,
           rwkv_k_a, rwkv_r_k, rwkv_gn_g, rwkv_gn_b, w_pa, w_pb, w_out, ln1_g, ln1_b, w_gate, w_up,
           w_down, ln2_g, ln2_b):
    y_prompt, y_sample = x_prompt, x_sample
    for layer in range(w_in.shape[0]):
        wl = w_in[layer]
        gate_start = S5_WIDTH + RWKV_IN_W
        p = {
            "w_u": _mx(wl[:, :S5_WIDTH]),
            "w_z": _mx(wl[:, S5_WIDTH:gate_start]),
            "w_gates": _mx(wl[:, gate_start:]),
            "s5_mats": _s5_chunk_matrices(s5_lam_re[layer], s5_lam_im[layer], s5_log_dt[layer],
                                          s5_b_re[layer], s5_b_im[layer], s5_c_re[layer], s5_c_im[layer],
                                          s5_d[layer]),
            "w_glu": _mx(s5_w_glu[layer]), "b_glu": _row(s5_b_glu[layer]),
            "mu": _row(rwkv_mu[layer]), "w0": rwkv_w0[layer].astype(F32),
            "w2": _mx(_pad_rank(rwkv_w2[layer])), "a0": rwkv_a0[layer].astype(F32),
            "a2": _mx(_pad_rank(rwkv_a2[layer])), "g2": _mx(rwkv_g2[layer]),
            "k_k": _row(rwkv_k_k[layer]), "k_a": _row(rwkv_k_a[layer]), "r_k": _row(rwkv_r_k[layer]),
            "gn_g": _row(rwkv_gn_g[layer]), "gn_b": _row(rwkv_gn_b[layer]),
            "w_pa": _mx(w_pa[layer]), "w_pb": _mx(w_pb[layer]), "w_out": _mx(w_out[layer]),
            "ln1_g": _row(ln1_g[layer]), "ln1_b": _row(ln1_b[layer]),
            "w_gate": _mx(w_gate[layer]), "w_up": _mx(w_up[layer]), "w_down": _mx(w_down[layer]),
            "ln2_g": _row(ln2_g[layer]), "ln2_b": _row(ln2_b[layer]),
        }
        y_prompt = _encoder_layer(y_prompt, p)
        y_sample = _encoder_layer(y_sample, p)
    return (y_prompt, y_sample)
```

```python
import functools

import jax
import jax.numpy as jnp
from jax import lax
from jax.experimental import pallas as pl
from jax.experimental.pallas import tpu as pltpu

F32 = jnp.float32
MXU_DTYPE = jnp.bfloat16

D_MODEL = 1024
S5_WIDTH = 512
S5_GROUP = 16
S5_GROUPS = S5_WIDTH // S5_GROUP
S5_STATE = 64
RWKV_WIDTH = 512
RWKV_HEAD = 64
DECAY_RANK = 64
ICLR_RANK = 64
GATE_RANK = 128
RWKV_IN_W = 3 * RWKV_WIDTH + 2 * DECAY_RANK + 2 * ICLR_RANK + GATE_RANK
D_FF = 2816
DEPTH = 1
ALPHA = (2.0 * DEPTH) ** 0.25
LN_EPS = 1e-5
GN_EPS = 64e-5
NORM_EPS = 1e-12

S5_CHUNK = 16
S5_CW = S5_CHUNK * S5_GROUP
S5_SLOT = 128
S5_LANES = 128
S5_SEQ_TILE = 8
RWKV_CHUNK = 64
PAIR = 2 * RWKV_HEAD
N_PAIRS = RWKV_WIDTH // PAIR
RWKV_SEQ_TILE = 2
TOKEN_TILE = 512
VMEM_LIMIT = 56 * 1024 * 1024


def _mx(a):
    return a.astype(MXU_DTYPE)


def _dot(a, b):
    return jnp.dot(_mx(a), _mx(b), preferred_element_type=F32)


def _dot_nt(a, b):
    return lax.dot_general(_mx(a), _mx(b), (((1,), (1,)), ((), ())), preferred_element_type=F32)


def _split2(x):
    hi = x.astype(MXU_DTYPE)
    return hi, (x - hi.astype(F32)).astype(MXU_DTYPE)


def _dot_exact_rhs(x, m):
    hi, lo = _split2(x)
    return jnp.dot(hi, m, preferred_element_type=F32) + jnp.dot(lo, m, preferred_element_type=F32)


def _dot_exact_lhs(m, x):
    hi, lo = _split2(x)
    return jnp.dot(m, hi, preferred_element_type=F32) + jnp.dot(m, lo, preferred_element_type=F32)


def _layer_norm(v, g, b):
    mu = jnp.mean(v, axis=-1, keepdims=True)
    d = v - mu
    var = jnp.mean(d * d, axis=-1, keepdims=True)
    return d * lax.rsqrt(var + LN_EPS) * g + b


def _gelu_tanh(v):
    return 0.5 * v * (1.0 + jnp.tanh(0.7978845608028654 * (v + 0.044715 * (v * v * v))))


def _const_spec(shape):
    nd = len(shape)
    return pl.BlockSpec(shape, lambda *_: (0,) * nd)


def _proj_kernel(x_ref, xp_ref, xn_ref, wu_ref, wz_ref, mu_ref, u_ref, z_ref, *, seqlen):
    tm = x_ref.shape[0]
    x = x_ref[...]
    u_ref[...] = jnp.dot(_mx(x), wu_ref[...], preferred_element_type=F32)
    x_ext = jnp.concatenate([xp_ref[...], x, xn_ref[...]], axis=0)
    z_ext = jnp.dot(_mx(x_ext), wz_ref[...], preferred_element_type=F32)
    first = pl.program_id(0) * tm
    row = lax.broadcasted_iota(jnp.int32, (tm, 1), 0)
    has_prev = (row > 0) | (first % seqlen != 0)
    has_next = (row < tm - 1) | ((first + tm) % seqlen != 0)
    z = z_ext[8:8 + tm]
    prev = jnp.where(has_prev, pltpu.roll(z_ext, 1, 0)[8:8 + tm], 0.0)
    nxt = jnp.where(has_next, pltpu.roll(z_ext, tm + 15, 0)[8:8 + tm], 0.0)
    z_ref[...] = (z + (0.5 * (prev + nxt) - z) * mu_ref[...]).astype(z_ref.dtype)


def _proj(x2, wu, wz, mu, seqlen):
    n = x2.shape[0]
    tm = min(TOKEN_TILE, seqlen)
    blocks8 = tm // 8
    last8 = n // 8 - 1
    return pl.pallas_call(
        functools.partial(_proj_kernel, seqlen=seqlen),
        name="proj",
        grid=(n // tm,),
        in_specs=[pl.BlockSpec((tm, D_MODEL), lambda i: (i, 0)),
                  pl.BlockSpec((8, D_MODEL), lambda i: (jnp.maximum(i * blocks8 - 1, 0), 0)),
                  pl.BlockSpec((8, D_MODEL), lambda i: (jnp.minimum((i + 1) * blocks8, last8), 0)),
                  _const_spec(wu.shape), _const_spec(wz.shape), _const_spec(mu.shape)],
        out_specs=[pl.BlockSpec((tm, S5_WIDTH), lambda i: (i, 0)),
                   pl.BlockSpec((tm, RWKV_IN_W), lambda i: (i, 0))],
        out_shape=[jax.ShapeDtypeStruct((n, S5_WIDTH), F32),
                   jax.ShapeDtypeStruct((n, RWKV_IN_W), MXU_DTYPE)],
        compiler_params=pltpu.CompilerParams(dimension_semantics=("arbitrary",),
                                             vmem_limit_bytes=VMEM_LIMIT),
    )(x2, x2, x2, wu, wz, mu)


def _s5_chunk_matrices(lam_re, lam_im, log_dt, b_re, b_im, c_re, c_im, d_skip):
    hp = lax.Precision.HIGHEST
    t = S5_CHUNK
    g = S5_GROUPS
    dt = jnp.exp(log_dt.astype(F32))[..., None]
    lre = lam_re.astype(F32)
    lim = lam_im.astype(F32)
    zr = lre * dt
    zi = lim * dt
    n = jnp.arange(t + 1, dtype=F32)[:, None, None, None]
    mag = jnp.exp(n * zr)
    pr = mag * jnp.cos(n * zi)
    pi = mag * jnp.sin(n * zi)
    den = lre * lre + lim * lim
    lbr = pr[1] - 1.0
    lbi = pi[1]
    qr = (lbr * lre + lbi * lim) / den
    qi = (lbi * lre - lbr * lim) / den
    bre = b_re.astype(F32)
    bim = b_im.astype(F32)
    bbr = qr[..., None] * bre - qi[..., None] * bim
    bbi = qr[..., None] * bim + qi[..., None] * bre
    wr = pr[..., None] * bbr - pi[..., None] * bbi
    wi = pr[..., None] * bbi + pi[..., None] * bbr
    cre = c_re.astype(F32)
    cim = c_im.astype(F32)
    kern = (jnp.einsum('dghp,ndgpk->ndghk', cre, wr[:t], precision=hp)
            - jnp.einsum('dghp,ndgpk->ndghk', cim, wi[:t], precision=hp))
    jj = jnp.arange(t)[:, None]
    ii = jnp.arange(t)[None, :]
    lag = jnp.arange(t)
    sel_f = ((ii - jj)[..., None] == lag).astype(F32)
    sel_b = ((jj - ii)[..., None] == lag).astype(F32)
    resp = (jnp.einsum('jin,nghk->jighk', sel_f, kern[:, 0], precision=hp)
            + jnp.einsum('jin,nghk->jighk', sel_b, kern[:, 1], precision=hp))
    t_mat = jnp.transpose(resp, (2, 0, 4, 1, 3)).reshape(g, S5_CW, S5_CW)

    def rows_jh(a):
        a = jnp.transpose(a, (1, 0, 3, 2)).reshape(g, S5_CW, S5_STATE)
        return jnp.pad(a, ((0, 0), (0, 0), (0, S5_SLOT - S5_STATE)))

    m_in = jnp.concatenate([rows_jh(wr[:t, 0][::-1]), rows_jh(wi[:t, 0][::-1]),
                            rows_jh(wr[:t, 1]), rows_jh(wi[:t, 1])], axis=-1)

    def cols_ih(a):
        a = jnp.transpose(a, (1, 3, 0, 2)).reshape(g, S5_STATE, S5_CW)
        return jnp.pad(a, ((0, 0), (0, S5_SLOT - S5_STATE), (0, 0)))

    pf_r, pf_i = pr[1:, 0][:, :, None, :], pi[1:, 0][:, :, None, :]
    pb_r, pb_i = pr[1:, 1][::-1][:, :, None, :], pi[1:, 1][::-1][:, :, None, :]
    m_out = jnp.concatenate([
        cols_ih(cre[0][None] * pf_r - cim[0][None] * pf_i),
        cols_ih(-(cre[0][None] * pf_i + cim[0][None] * pf_r)),
        cols_ih(cre[1][None] * pb_r - cim[1][None] * pb_i),
        cols_ih(-(cre[1][None] * pb_i + cim[1][None] * pb_r))], axis=1)
    a_step = jnp.pad(jnp.stack([pr[t, 0], pi[t, 0], pr[t, 1], pi[t, 1]], axis=1),
                     ((0, 0), (0, 0), (0, S5_SLOT - S5_STATE)))

    def halves(x, slot_axis):
        x = x.reshape((g // 2, 2) + x.shape[1:])
        return x[:, 0], jnp.roll(x[:, 1], S5_STATE, axis=slot_axis)

    in_a, in_b = halves(m_in.reshape(g, S5_CW, 4, S5_SLOT), -1)
    m_in = jnp.concatenate([in_a, in_b], axis=1).reshape(g // 2, 2 * S5_CW, 4 * S5_SLOT)
    out_a, out_b = halves(m_out.reshape(g, 4, S5_SLOT, S5_CW), -2)
    m_out = jnp.concatenate([out_a, out_b], axis=-1).reshape(g // 2, 4 * S5_SLOT, 2 * S5_CW)
    step_a, step_b = halves(a_step, -1)
    return (_mx(t_mat), _mx(m_in), _mx(m_out), step_a + step_b, d_skip.astype(F32).reshape(1, S5_WIDTH))


def _block_transpose8(xs):
    xs = list(xs)
    lane_block = lax.broadcasted_iota(jnp.int32, (1, S5_LANES), 1) // S5_GROUP
    for s in (4, 2, 1):
        keep = (lane_block & s) == 0
        for a in range(8):
            if a & s:
                continue
            lo, hi = xs[a], xs[a | s]
            xs[a] = jnp.where(keep, lo, pltpu.roll(hi, S5_GROUP * s, 1))
            xs[a | s] = jnp.where(keep, pltpu.roll(lo, S5_LANES - S5_GROUP * s, 1), hi)
    return xs


def _s5_kernel(u_ref, t_ref, min_ref, mout_ref, a_ref, d_ref, y_ref, q_ref, yg_ref, sin_ref, xs_ref, *, n_chunks):
    bt = S5_SEQ_TILE
    rows = bt * n_chunks
    n_groups = S5_LANES // S5_GROUP
    slot = S5_SLOT

    for v in range(2):
        pieces = [u_ref[pl.ds(8 * v + a, rows, stride=S5_CHUNK), :] for a in range(8)]
        for gi, q in enumerate(_block_transpose8(pieces)):
            q_ref[gi, :, v * S5_LANES:(v + 1) * S5_LANES] = _mx(q)

    zero = jnp.zeros((bt, slot), F32)
    for gp in range(n_groups // 2):
        q = (q_ref[2 * gp], q_ref[2 * gp + 1])
        s_in = jnp.dot(jnp.concatenate(q, axis=1), min_ref[gp], preferred_element_type=F32)
        for c in range(4):
            sin_ref[c] = s_in[:, c * slot:(c + 1) * slot]
        a = a_ref[gp]
        af_re, af_im, ab_re, ab_im = a[0:1], a[1:2], a[2:3], a[3:4]

        def step(k, carry):
            f_re, f_im, b_re, b_im = carry
            rk = pl.ds(k, bt, stride=n_chunks)
            rr = pl.ds(n_chunks - 1 - k, bt, stride=n_chunks)
            sf_re = sin_ref[0, rk, :]
            sf_im = sin_ref[1, rk, :]
            sb_re = sin_ref[2, rr, :]
            sb_im = sin_ref[3, rr, :]
            xs_ref[0, rk, :] = f_re
            xs_ref[1, rk, :] = f_im
            xs_ref[2, rr, :] = b_re
            xs_ref[3, rr, :] = b_im
            return (af_re * f_re - af_im * f_im + sf_re, af_re * f_im + af_im * f_re + sf_im,
                    ab_re * b_re - ab_im * b_im + sb_re, ab_re * b_im + ab_im * b_re + sb_im)

        lax.fori_loop(0, n_chunks, step, (zero, zero, zero, zero), unroll=4)
        states = jnp.concatenate([_mx(xs_ref[c]) for c in range(4)], axis=1)
        from_state = jnp.dot(states, mout_ref[gp], preferred_element_type=F32)
        for e in range(2):
            yg_ref[2 * gp + e] = (jnp.dot(q[e], t_ref[2 * gp + e], preferred_element_type=F32)
                                  + from_state[:, e * S5_CW:(e + 1) * S5_CW])

    d = d_ref[...]
    for v in range(2):
        halves = [yg_ref[gi, :, v * S5_LANES:(v + 1) * S5_LANES] for gi in range(n_groups)]
        for a, y in enumerate(_block_transpose8(halves)):
            token_rows = pl.ds(8 * v + a, rows, stride=S5_CHUNK)
            y_ref[token_rows, :] = y + d * u_ref[token_rows, :]


def _s5_branch(u2, mats, seqlen):
    t_mat, m_in, m_out, a_step, d_row = mats
    n = u2.shape[0]
    n_chunks = seqlen // S5_CHUNK
    n_groups = S5_LANES // S5_GROUP
    tile_rows = S5_SEQ_TILE * seqlen
    pairs = S5_SEQ_TILE * n_chunks
    once = pl.Buffered(1)
    return pl.pallas_call(
        functools.partial(_s5_kernel, n_chunks=n_chunks),
        name="s5_scan",
        grid=(S5_WIDTH // S5_LANES, n // tile_rows),
        in_specs=[pl.BlockSpec((tile_rows, S5_LANES), lambda s, b: (b, s)),
                  pl.BlockSpec((n_groups, S5_CW, S5_CW), lambda s, b: (s, 0, 0), pipeline_mode=once),
                  pl.BlockSpec((n_groups // 2, 2 * S5_CW, 4 * S5_SLOT), lambda s, b: (s, 0, 0), pipeline_mode=once),
                  pl.BlockSpec((n_groups // 2, 4 * S5_SLOT, 2 * S5_CW), lambda s, b: (s, 0, 0), pipeline_mode=once),
                  pl.BlockSpec((n_groups // 2, 4, S5_SLOT), lambda s, b: (s, 0, 0)),
                  pl.BlockSpec((1, S5_LANES), lambda s, b: (0, s))],
        out_specs=pl.BlockSpec((tile_rows, S5_LANES), lambda s, b: (b, s), pipeline_mode=once),
        out_shape=jax.ShapeDtypeStruct((n, S5_WIDTH), F32),
        scratch_shapes=[pltpu.VMEM((n_groups, pairs, S5_CW), MXU_DTYPE),
                        pltpu.VMEM((n_groups, pairs, S5_CW), F32),
                        pltpu.VMEM((4, pairs, S5_SLOT), F32),
                        pltpu.VMEM((4, pairs, S5_SLOT), F32)],
        compiler_params=pltpu.CompilerParams(dimension_semantics=("arbitrary", "arbitrary"),
                                             vmem_limit_bytes=VMEM_LIMIT),
    )(u2, t_mat, m_in, m_out, a_step, d_row)


def _rwkv_kernel(z_ref, w0_ref, w2_ref, a0_ref, a2_ref, g2_ref, kk_ref, ka_ref, rk_ref,
                 gng_ref, gnb_ref, o_ref, s_ref, y_ref, pa_ref, pg_ref, *, seqlen):
    n_seq = z_ref.shape[0]
    t = RWKV_CHUNK
    n_chunks = seqlen // t
    w = RWKV_WIDTH

    tri_r = lax.broadcasted_iota(jnp.int32, (t, t), 0)
    tri_c = lax.broadcasted_iota(jnp.int32, (t, t), 1)
    pr = lax.broadcasted_iota(jnp.int32, (PAIR, PAIR), 0)
    pc = lax.broadcasted_iota(jnp.int32, (PAIR, PAIR), 1)
    seg_ones = jnp.where(pr // RWKV_HEAD == pc // RWKV_HEAD, 1.0, 0.0).astype(MXU_DTYPE)

    def head_sums(a, op, dot):
        return jnp.concatenate([dot(a[:, p * PAIR:(p + 1) * PAIR], op) for p in range(N_PAIRS)], axis=1)

    eye = jnp.where(pr == pc, 1.0, 0.0).astype(F32)
    first_head = lax.broadcasted_iota(jnp.int32, (t, PAIR), 1) < RWKV_HEAD
    cum_ops = (jnp.where(tri_r >= tri_c, 1.0, 0.0).astype(MXU_DTYPE),
               jnp.where(tri_r <= tri_c, 1.0, 0.0).astype(MXU_DTYPE))
    sr = lax.broadcasted_iota(jnp.int32, (t, PAIR), 0)
    scol = lax.broadcasted_iota(jnp.int32, (t, PAIR), 1) % t
    strict = (sr > scol, sr < scol)
    incl = (sr >= scol, sr <= scol)

    k_k = kk_ref[...]
    k_a = ka_ref[...]
    r_k = rk_ref[...]

    s_ref[...] = jnp.zeros_like(s_ref)
    y_ref[...] = jnp.zeros_like(y_ref)
    o_ref[...] = jnp.zeros_like(o_ref)

    def stack(ap):
        return jnp.concatenate([jnp.where(first_head, ap, 0.0), jnp.where(first_head, 0.0, ap)], axis=0)

    names = ("a", "r", "b", "k", "v", "be", "ke")
    stacked = ("b", "k", "v", "be", "ke")

    def prep_gen(ci_f, ci_b, valid):
        for q, d, ci in [(q, d, ci) for q in range(n_seq) for d, ci in ((0, ci_f), (1, ci_b))]:
            rows = pl.ds(pl.multiple_of(ci * t, t), t)
            zs = z_ref[q, rows, 0:3 * w + 2 * DECAY_RANK + 2 * ICLR_RANK].astype(F32)
            r = zs[:, 0:w]
            k = zs[:, w:2 * w]
            v = zs[:, 2 * w:3 * w]
            w_low = zs[:, 3 * w:3 * w + 2 * DECAY_RANK]
            a_low = zs[:, 3 * w + 2 * DECAY_RANK:3 * w + 2 * DECAY_RANK + 2 * ICLR_RANK]
            w_log = w0_ref[d:d + 1, :] + _dot(jnp.tanh(w_low), w2_ref[d])
            yield
            w_log = -jnp.logaddexp(-w_log, 0.0) - 0.5
            logw = -jnp.exp(w_log)
            iclr = jax.nn.sigmoid(a0_ref[d:d + 1, :] + _dot(a_low, a2_ref[d]))
            yield
            kk = k * k_k
            nrm = jnp.sqrt(head_sums(kk * kk, seg_ones, _dot))
            kk = kk / jnp.maximum(nrm, NORM_EPS)
            yield
            k_dir = k * (1.0 + (iclr - 1.0) * k_a)
            o_ref[q, rows, :] += head_sums(r * k_dir * r_k, seg_ones, _dot_exact_rhs) * (v * valid)
            yield
            cum = _dot_exact_lhs(cum_ops[d], logw)
            last = cum[t - 1:t, :] if d == 0 else cum[0:1, :]
            yield
            e_neg = jnp.exp(-cum)
            b_dir = kk * iclr
            e_end = jnp.exp(last - cum)
            vals = (-kk * jnp.exp(cum - logw), r * jnp.exp(cum), b_dir * e_neg, k_dir * e_neg, v,
                    b_dir * e_end, k_dir * e_end)
            yield
            for j, val in enumerate(vals):
                pa_ref[q, d, j] = val
            pg_ref[q, d, 0:1, :] = jnp.exp(last)
            yield

    chains = [(q, d, p) for q in range(n_seq) for d in range(2) for p in range(N_PAIRS)]

    def chain_gen(i):
        rows = (pl.ds(pl.multiple_of(i * t, t), t), pl.ds(pl.multiple_of((n_chunks - 1 - i) * t, t), t))
        slab = {(q, d, p): {nm: pa_ref[q, d, j, :, p * PAIR:(p + 1) * PAIR] for j, nm in enumerate(names)}
                for q, d, p in chains}
        st = {c: {nm: _mx(stack(slab[c][nm])) for nm in stacked} for c in chains}
        g_end = {(q, d): pg_ref[q, d, 0:1, :] for q in range(n_seq) for d in range(2)}
        ar = {c: _mx(jnp.concatenate([slab[c]["a"], slab[c]["r"]], axis=0)) for c in chains}
        sc = {c: _dot_nt(ar[c], jnp.concatenate([st[c]["b"], st[c]["k"]], axis=0)) for c in chains}
        yield
        l_ab = {c: stack(jnp.where(strict[c[1]], sc[c][0:t, 0:PAIR], 0.0)) for c in chains}
        mix = {c: _mx(jnp.concatenate([jnp.where(strict[c[1]], sc[c][0:t, PAIR:2 * PAIR], 0.0),
                                       jnp.where(incl[c[1]], sc[c][t:2 * t, PAIR:2 * PAIR], 0.0)], axis=0))
               for c in chains}
        m_rb = {c: _mx(jnp.where(incl[c[1]], sc[c][t:2 * t, 0:PAIR], 0.0)) for c in chains}
        s_old = {c: s_ref[c] for c in chains}
        x = {c: _dot(jnp.concatenate([ar[c], mix[c]], axis=1),
                     jnp.concatenate([_mx(s_old[c].T), st[c]["v"]], axis=0)) for c in chains}
        yield
        m = {c: _dot(l_ab[c], l_ab[c]) for c in chains}
        inv = {c: eye + l_ab[c] for c in chains}
        yield
        for _ in range(4):
            mp = {c: _dot(m[c], jnp.concatenate([m[c], inv[c]], axis=1)) for c in chains}
            m = {c: mp[c][:, 0:PAIR] for c in chains}
            inv = {c: inv[c] + mp[c][:, PAIR:2 * PAIR] for c in chains}
            yield
        inv = {c: inv[c] + _dot(m[c], inv[c]) for c in chains}
        yield
        u = {c: _dot(inv[c], stack(x[c][0:t])) for c in chains}
        yield
        o = {c: x[c][t:2 * t] + _dot(m_rb[c], u[c]) for c in chains}
        yield
        for c in chains:
            q, d, p = c
            lanes = slice(p * PAIR, (p + 1) * PAIR)
            y_ref[q, rows[d], lanes] += o[c]
            uv_t = jnp.concatenate([u[c].T, st[c]["v"].astype(F32).T], axis=1)
            s_ref[c] = (s_old[c] * g_end[q, d][:, lanes]
                        + _dot(uv_t, jnp.concatenate([st[c]["be"], st[c]["ke"]], axis=0)))
        yield

    def interleave(main, side, side_per_main):
        for _ in main:
            for _ in range(side_per_main):
                next(side, None)
        for _ in side:
            pass

    prep_stages = 7 * 2 * n_seq
    chain_stages = 11

    def body(i, carry):
        nxt = jnp.minimum(i + 1, n_chunks - 1)
        valid = jnp.where(i + 1 < n_chunks, 1.0, 0.0)
        interleave(chain_gen(i), prep_gen(nxt, n_chunks - 1 - nxt, valid), -(-prep_stages // chain_stages))
        return carry

    for _ in prep_gen(0, n_chunks - 1, 1.0):
        pass
    lax.fori_loop(0, n_chunks, body, 0)

    blk = min(seqlen, 256)
    gn_g = gng_ref[...]
    gn_b = gnb_ref[...]
    seg_mean = (seg_ones.astype(F32) * (1.0 / RWKV_HEAD)).astype(MXU_DTYPE)

    def norm_body(i, carry):
        rows = pl.ds(pl.multiple_of(i * blk, blk), blk)
        for q in range(n_seq):
            y = y_ref[q, rows, :]
            dlt = y - head_sums(y, seg_mean, _dot_exact_rhs)
            var = head_sums(dlt * dlt, seg_mean, _dot)
            yn = dlt * lax.rsqrt(var + GN_EPS) * gn_g + gn_b
            gate = _dot(jax.nn.sigmoid(z_ref[q, rows, RWKV_IN_W - GATE_RANK:].astype(F32)), g2_ref[...])
            o_ref[q, rows, :] = (yn + o_ref[q, rows, :]) * gate
        return carry

    lax.fori_loop(0, seqlen // blk, norm_body, 0)


def _rwkv_branch(z, p):
    bsz, seqlen, _ = z.shape
    consts = [p["w0"], p["w2"], p["a0"], p["a2"], p["g2"], p["k_k"], p["k_a"], p["r_k"],
              p["gn_g"], p["gn_b"]]
    ns = RWKV_SEQ_TILE
    return pl.pallas_call(
        functools.partial(_rwkv_kernel, seqlen=seqlen),
        name="rwkv",
        grid=(bsz // ns,),
        in_specs=[pl.BlockSpec((ns, seqlen, RWKV_IN_W), lambda i: (i, 0, 0), pipeline_mode=pl.Buffered(1))]
                 + [_const_spec(c.shape) for c in consts],
        out_specs=pl.BlockSpec((ns, seqlen, RWKV_WIDTH), lambda i: (i, 0, 0)),
        out_shape=jax.ShapeDtypeStruct((bsz, seqlen, RWKV_WIDTH), F32),
        scratch_shapes=[pltpu.VMEM((ns, 2, N_PAIRS, PAIR, PAIR), F32),
                        pltpu.VMEM((ns, seqlen, RWKV_WIDTH), F32),
                        pltpu.VMEM((ns, 2, 7, RWKV_CHUNK, RWKV_WIDTH), F32),
                        pltpu.VMEM((ns, 2, 8, RWKV_WIDTH), F32)],
        compiler_params=pltpu.CompilerParams(dimension_semantics=("arbitrary",),
                                             vmem_limit_bytes=VMEM_LIMIT),
    )(z, *consts)


def _merge_kernel(x_ref, ys_ref, yr_ref, wg_ref, wglu_ref, bglu_ref, wpa_ref, wpb_ref, wout_ref,
                  g_ref, b_ref, h_ref):
    x = x_ref[...]
    xb = _mx(x)
    act = _gelu_tanh(ys_ref[...])
    out_a = act * jax.nn.sigmoid(_dot(act, wglu_ref[...]) + bglu_ref[...])
    gate_a = jax.nn.sigmoid(jnp.dot(xb, wg_ref[:, 0:D_MODEL], preferred_element_type=F32))
    merged = gate_a * _dot(out_a, wpa_ref[...])
    gate_b = jax.nn.sigmoid(jnp.dot(xb, wg_ref[:, D_MODEL:2 * D_MODEL], preferred_element_type=F32))
    merged = merged + gate_b * _dot(yr_ref[...], wpb_ref[...])
    h_ref[...] = _layer_norm(ALPHA * x + _dot(merged, wout_ref[...]), g_ref[...], b_ref[...])


def _merge(x2, ys, yr, p):
    n = x2.shape[0]
    tm = min(TOKEN_TILE, n)
    consts = [p["w_gates"], p["w_glu"], p["b_glu"], p["w_pa"], p["w_pb"], p["w_out"], p["ln1_g"], p["ln1_b"]]
    return pl.pallas_call(
        _merge_kernel,
        name="merge",
        grid=(n // tm,),
        in_specs=[pl.BlockSpec((tm, D_MODEL), lambda i: (i, 0)),
                  pl.BlockSpec((tm, S5_WIDTH), lambda i: (i, 0)),
                  pl.BlockSpec((tm, RWKV_WIDTH), lambda i: (i, 0))]
                 + [_const_spec(c.shape) for c in consts],
        out_specs=pl.BlockSpec((tm, D_MODEL), lambda i: (i, 0)),
        out_shape=jax.ShapeDtypeStruct((n, D_MODEL), F32),
        compiler_params=pltpu.CompilerParams(dimension_semantics=("arbitrary",),
                                             vmem_limit_bytes=VMEM_LIMIT),
    )(x2, ys, yr, *consts)


def _ffn_kernel(h_ref, wg_ref, wu_ref, wd_ref, g_ref, b_ref, y_ref):
    h = h_ref[...]
    hb = _mx(h)
    half = D_FF // 2
    acc = ALPHA * h
    for c in range(2):
        cols = slice(c * half, (c + 1) * half)
        gate = jnp.dot(hb, wg_ref[:, cols], preferred_element_type=F32)
        up = jnp.dot(hb, wu_ref[:, cols], preferred_element_type=F32)
        acc = acc + _dot(gate * jax.nn.sigmoid(gate) * up, wd_ref[cols, :])
    y_ref[...] = _layer_norm(acc, g_ref[...], b_ref[...])


def _ffn(h2, p):
    n = h2.shape[0]
    tm = min(TOKEN_TILE, n)
    consts = [p["w_gate"], p["w_up"], p["w_down"], p["ln2_g"], p["ln2_b"]]
    return pl.pallas_call(
        _ffn_kernel,
        name="ffn",
        grid=(n // tm,),
        in_specs=[pl.BlockSpec((tm, D_MODEL), lambda i: (i, 0))] + [_const_spec(c.shape) for c in consts],
        out_specs=pl.BlockSpec((tm, D_MODEL), lambda i: (i, 0)),
        out_shape=jax.ShapeDtypeStruct((n, D_MODEL), F32),
        compiler_params=pltpu.CompilerParams(dimension_semantics=("arbitrary",),
                                             vmem_limit_bytes=VMEM_LIMIT),
    )(h2, *consts)


def _pad_rank(w2):
    z = jnp.zeros_like(w2[0])
    return jnp.stack([jnp.concatenate([w2[0], z], axis=0), jnp.concatenate([z, w2[1]], axis=0)])


def _row(v):
    return v.astype(F32).reshape(1, -1)


def _encoder_layer(x, p):
    bsz, seqlen, _ = x.shape
    x2 = x.reshape(bsz * seqlen, D_MODEL)
    u, z = _proj(x2, p["w_u"], p["w_z"], p["mu"], seqlen)
    ys = _s5_branch(u, p["s5_mats"], seqlen)
    yr = _rwkv_branch(z.reshape(bsz, seqlen, RWKV_IN_W), p).reshape(bsz * seqlen, RWKV_WIDTH)
    h = _merge(x2, ys, yr, p)
    return _ffn(h, p).reshape(bsz, seqlen, D_MODEL)


def kernel(x_prompt, x_sample, w_in, s5_lam_re, s5_lam_im, s5_log_dt, s5_b_re, s5_b_im, s5_c_re, s5_c_im,
           s5_d, s5_w_glu, s5_b_glu, rwkv_mu, rwkv_w0, rwkv_w2, rwkv_a0, rwkv_a2, rwkv_g2, rwkv_k_k,
           rwkv_k_a, rwkv_r_k, rwkv_gn_g, rwkv_gn_b, w_pa, w_pb, w_out, ln1_g, ln1_b, w_gate, w_up,
           w_down, ln2_g, ln2_b):
    y_prompt, y_sample = x_prompt, x_sample
    for layer in range(w_in.shape[0]):
        wl = w_in[layer]
        gate_start = S5_WIDTH + RWKV_IN_W
        p = {
            "w_u": _mx(wl[:, :S5_WIDTH]),
            "w_z": _mx(wl[:, S5_WIDTH:gate_start]),
            "w_gates": _mx(wl[:, gate_start:]),
            "s5_mats": _s5_chunk_matrices(s5_lam_re[layer], s5_lam_im[layer], s5_log_dt[layer],
                                          s5_b_re[layer], s5_b_im[layer], s5_c_re[layer], s5_c_im[layer],
                                          s5_d[layer]),
            "w_glu": _mx(s5_w_glu[layer]), "b_glu": _row(s5_b_glu[layer]),
            "mu": _row(rwkv_mu[layer]), "w0": rwkv_w0[layer].astype(F32),
            "w2": _mx(_pad_rank(rwkv_w2[layer])), "a0": rwkv_a0[layer].astype(F32),
            "a2": _mx(_pad_rank(rwkv_a2[layer])), "g2": _mx(rwkv_g2[layer]),
            "k_k": _row(rwkv_k_k[layer]), "k_a": _row(rwkv_k_a[layer]), "r_k": _row(rwkv_r_k[layer]),
            "gn_g": _row(rwkv_gn_g[layer]), "gn_b": _row(rwkv_gn_b[layer]),
            "w_pa": _mx(w_pa[layer]), "w_pb": _mx(w_pb[layer]), "w_out": _mx(w_out[layer]),
            "ln1_g": _row(ln1_g[layer]), "ln1_b": _row(ln1_b[layer]),
            "w_gate": _mx(w_gate[layer]), "w_up": _mx(w_up[layer]), "w_down": _mx(w_down[layer]),
            "ln2_g": _row(ln2_g[layer]), "ln2_b": _row(ln2_b[layer]),
        }
        y_prompt = _encoder_layer(y_prompt, p)
        y_sample = _encoder_layer(y_sample, p)
    return (y_prompt, y_sample)
```

```python
import functools

import jax
import jax.numpy as jnp
from jax import lax
from jax.experimental import pallas as pl
from jax.experimental.pallas import tpu as pltpu

F32 = jnp.float32
MXU_DTYPE = jnp.bfloat16

D_MODEL = 1024
S5_WIDTH = 512
S5_GROUP = 16
S5_GROUPS = S5_WIDTH // S5_GROUP
S5_STATE = 64
RWKV_WIDTH = 512
RWKV_HEAD = 64
DECAY_RANK = 64
ICLR_RANK = 64
GATE_RANK = 128
RWKV_IN_W = 3 * RWKV_WIDTH + 2 * DECAY_RANK + 2 * ICLR_RANK + GATE_RANK
D_FF = 2816
DEPTH = 1
ALPHA = (2.0 * DEPTH) ** 0.25
LN_EPS = 1e-5
GN_EPS = 64e-5
NORM_EPS = 1e-12

S5_CHUNK = 16
S5_CW = S5_CHUNK * S5_GROUP
S5_SLOT = 128
S5_LANES = 128
S5_SEQ_TILE = 8
RWKV_CHUNK = 64
PAIR = 2 * RWKV_HEAD
N_PAIRS = RWKV_WIDTH // PAIR
RWKV_SEQ_TILE = 2
TOKEN_TILE = 512
FFN_SPLIT = 1536
VMEM_LIMIT = 56 * 1024 * 1024


def _mx(a):
    return a.astype(MXU_DTYPE)


def _dot(a, b):
    return jnp.dot(_mx(a), _mx(b), preferred_element_type=F32)


def _dot_nt(a, b):
    return lax.dot_general(_mx(a), _mx(b), (((1,), (1,)), ((), ())), preferred_element_type=F32)


def _split2(x):
    hi = x.astype(MXU_DTYPE)
    return hi, (x - hi.astype(F32)).astype(MXU_DTYPE)


def _dot_exact_lhs(m, x):
    hi, lo = _split2(x)
    return jnp.dot(m, hi, preferred_element_type=F32) + jnp.dot(m, lo, preferred_element_type=F32)


def _layer_norm(v, g, b):
    mu = jnp.mean(v, axis=-1, keepdims=True)
    d = v - mu
    var = jnp.mean(d * d, axis=-1, keepdims=True)
    return d * lax.rsqrt(var + LN_EPS) * g + b


def _gelu_tanh(v):
    return 0.5 * v * (1.0 + jnp.tanh(0.7978845608028654 * (v + 0.044715 * (v * v * v))))


def _const_spec(shape):
    nd = len(shape)
    return pl.BlockSpec(shape, lambda *_: (0,) * nd)


def _proj_kernel(x_ref, xp_ref, xn_ref, wu_ref, wz_ref, mu_ref, u_ref, z_ref, *, seqlen):
    tm = x_ref.shape[0]
    x = x_ref[...]
    u_ref[...] = jnp.dot(_mx(x), wu_ref[...], preferred_element_type=F32)
    x_ext = _mx(jnp.concatenate([xp_ref[...], x, xn_ref[...]], axis=0))
    first = pl.program_id(0) * tm
    row = lax.broadcasted_iota(jnp.int32, (tm, 1), 0)
    has_prev = (row > 0) | (first % seqlen != 0)
    has_next = (row < tm - 1) | ((first + tm) % seqlen != 0)
    z_ext = jnp.dot(x_ext, wz_ref[...], preferred_element_type=F32)
    z = z_ext[8:8 + tm]
    prev = jnp.where(has_prev, pltpu.roll(z_ext, 1, 0)[8:8 + tm], 0.0)
    nxt = jnp.where(has_next, pltpu.roll(z_ext, tm + 15, 0)[8:8 + tm], 0.0)
    z_ref[...] = (z + (0.5 * (prev + nxt) - z) * mu_ref[...]).astype(z_ref.dtype)


def _proj(x2, wu, wz, mu, seqlen):
    n = x2.shape[0]
    tm = min(TOKEN_TILE, seqlen)
    blocks8 = tm // 8
    last8 = n // 8 - 1
    return pl.pallas_call(
        functools.partial(_proj_kernel, seqlen=seqlen),
        name="proj",
        grid=(n // tm,),
        in_specs=[pl.BlockSpec((tm, D_MODEL), lambda i: (i, 0)),
                  pl.BlockSpec((8, D_MODEL), lambda i: (jnp.maximum(i * blocks8 - 1, 0), 0)),
                  pl.BlockSpec((8, D_MODEL), lambda i: (jnp.minimum((i + 1) * blocks8, last8), 0)),
                  _const_spec(wu.shape), _const_spec(wz.shape), _const_spec(mu.shape)],
        out_specs=[pl.BlockSpec((tm, S5_WIDTH), lambda i: (i, 0)),
                   pl.BlockSpec((tm, RWKV_IN_W), lambda i: (i, 0))],
        out_shape=[jax.ShapeDtypeStruct((n, S5_WIDTH), F32),
                   jax.ShapeDtypeStruct((n, RWKV_IN_W), MXU_DTYPE)],
        compiler_params=pltpu.CompilerParams(dimension_semantics=("arbitrary",),
                                             vmem_limit_bytes=VMEM_LIMIT),
    )(x2, x2, x2, wu, wz, mu)


def _s5_chunk_matrices(lam_re, lam_im, log_dt, b_re, b_im, c_re, c_im, d_skip):
    hp = lax.Precision.HIGHEST
    t = S5_CHUNK
    g = S5_GROUPS
    dt = jnp.exp(log_dt.astype(F32))[..., None]
    lre = lam_re.astype(F32)
    lim = lam_im.astype(F32)
    zr = lre * dt
    zi = lim * dt
    n = jnp.arange(t + 1, dtype=F32)[:, None, None, None]
    mag = jnp.exp(n * zr)
    pr = mag * jnp.cos(n * zi)
    pi = mag * jnp.sin(n * zi)
    den = lre * lre + lim * lim
    lbr = pr[1] - 1.0
    lbi = pi[1]
    qr = (lbr * lre + lbi * lim) / den
    qi = (lbi * lre - lbr * lim) / den
    bre = b_re.astype(F32)
    bim = b_im.astype(F32)
    bbr = qr[..., None] * bre - qi[..., None] * bim
    bbi = qr[..., None] * bim + qi[..., None] * bre
    wr = pr[..., None] * bbr - pi[..., None] * bbi
    wi = pr[..., None] * bbi + pi[..., None] * bbr
    cre = c_re.astype(F32)
    cim = c_im.astype(F32)
    kern = (jnp.einsum('dghp,ndgpk->ndghk', cre, wr[:t], precision=hp)
            - jnp.einsum('dghp,ndgpk->ndghk', cim, wi[:t], precision=hp))
    jj = jnp.arange(t)[:, None]
    ii = jnp.arange(t)[None, :]
    lag = jnp.arange(t)
    sel_f = ((ii - jj)[..., None] == lag).astype(F32)
    sel_b = ((jj - ii)[..., None] == lag).astype(F32)
    resp = (jnp.einsum('jin,nghk->jighk', sel_f, kern[:, 0], precision=hp)
            + jnp.einsum('jin,nghk->jighk', sel_b, kern[:, 1], precision=hp))
    t_mat = jnp.transpose(resp, (2, 0, 4, 1, 3)).reshape(g, S5_CW, S5_CW)

    def rows_jh(a):
        a = jnp.transpose(a, (1, 0, 3, 2)).reshape(g, S5_CW, S5_STATE)
        return jnp.pad(a, ((0, 0), (0, 0), (0, S5_SLOT - S5_STATE)))

    m_in = jnp.concatenate([rows_jh(wr[:t, 0][::-1]), rows_jh(wi[:t, 0][::-1]),
                            rows_jh(wr[:t, 1]), rows_jh(wi[:t, 1])], axis=-1)

    def cols_ih(a):
        a = jnp.transpose(a, (1, 3, 0, 2)).reshape(g, S5_STATE, S5_CW)
        return jnp.pad(a, ((0, 0), (0, S5_SLOT - S5_STATE), (0, 0)))

    pf_r, pf_i = pr[1:, 0][:, :, None, :], pi[1:, 0][:, :, None, :]
    pb_r, pb_i = pr[1:, 1][::-1][:, :, None, :], pi[1:, 1][::-1][:, :, None, :]
    m_out = jnp.concatenate([
        cols_ih(cre[0][None] * pf_r - cim[0][None] * pf_i),
        cols_ih(-(cre[0][None] * pf_i + cim[0][None] * pf_r)),
        cols_ih(cre[1][None] * pb_r - cim[1][None] * pb_i),
        cols_ih(-(cre[1][None] * pb_i + cim[1][None] * pb_r))], axis=1)
    a_step = jnp.pad(jnp.stack([pr[t, 0], pi[t, 0], pr[t, 1], pi[t, 1]], axis=1),
                     ((0, 0), (0, 0), (0, S5_SLOT - S5_STATE)))

    def halves(x, slot_axis):
        x = x.reshape((g // 2, 2) + x.shape[1:])
        return x[:, 0], jnp.roll(x[:, 1], S5_STATE, axis=slot_axis)

    in_a, in_b = halves(m_in.reshape(g, S5_CW, 4, S5_SLOT), -1)
    m_in = jnp.concatenate([in_a, in_b], axis=1).reshape(g // 2, 2 * S5_CW, 4 * S5_SLOT)
    out_a, out_b = halves(m_out.reshape(g, 4, S5_SLOT, S5_CW), -2)
    m_out = jnp.concatenate([out_a, out_b], axis=-1).reshape(g // 2, 4 * S5_SLOT, 2 * S5_CW)
    step_a, step_b = halves(a_step, -1)
    return (_mx(t_mat), _mx(m_in), _mx(m_out), step_a + step_b, d_skip.astype(F32).reshape(1, S5_WIDTH))


def _block_transpose8(xs):
    xs = list(xs)
    lane_block = lax.broadcasted_iota(jnp.int32, (1, S5_LANES), 1) // S5_GROUP
    for s in (4, 2, 1):
        keep = (lane_block & s) == 0
        for a in range(8):
            if a & s:
                continue
            lo, hi = xs[a], xs[a | s]
            xs[a] = jnp.where(keep, lo, pltpu.roll(hi, S5_GROUP * s, 1))
            xs[a | s] = jnp.where(keep, pltpu.roll(lo, S5_LANES - S5_GROUP * s, 1), hi)
    return xs


def _s5_kernel(u_ref, t_ref, min_ref, mout_ref, a_ref, d_ref, y_ref, q_ref, yg_ref, sin_ref, xs_ref, *, n_chunks):
    bt = S5_SEQ_TILE
    rows = bt * n_chunks
    n_groups = S5_LANES // S5_GROUP
    slot = S5_SLOT

    for v in range(2):
        pieces = [u_ref[pl.ds(8 * v + a, rows, stride=S5_CHUNK), :] for a in range(8)]
        for gi, q in enumerate(_block_transpose8(pieces)):
            q_ref[gi, :, v * S5_LANES:(v + 1) * S5_LANES] = _mx(q)

    zero = jnp.zeros((bt, slot), F32)
    for gp in range(n_groups // 2):
        q = (q_ref[2 * gp], q_ref[2 * gp + 1])
        s_in = jnp.dot(jnp.concatenate(q, axis=1), min_ref[gp], preferred_element_type=F32)
        for c in range(4):
            sin_ref[c] = s_in[:, c * slot:(c + 1) * slot]
        a = a_ref[gp]
        af_re, af_im, ab_re, ab_im = a[0:1], a[1:2], a[2:3], a[3:4]

        def step(k, carry):
            f_re, f_im, b_re, b_im = carry
            rk = pl.ds(k, bt, stride=n_chunks)
            rr = pl.ds(n_chunks - 1 - k, bt, stride=n_chunks)
            sf_re = sin_ref[0, rk, :]
            sf_im = sin_ref[1, rk, :]
            sb_re = sin_ref[2, rr, :]
            sb_im = sin_ref[3, rr, :]
            xs_ref[0, rk, :] = f_re
            xs_ref[1, rk, :] = f_im
            xs_ref[2, rr, :] = b_re
            xs_ref[3, rr, :] = b_im
            return (af_re * f_re - af_im * f_im + sf_re, af_re * f_im + af_im * f_re + sf_im,
                    ab_re * b_re - ab_im * b_im + sb_re, ab_re * b_im + ab_im * b_re + sb_im)

        lax.fori_loop(0, n_chunks, step, (zero, zero, zero, zero), unroll=4)
        states = jnp.concatenate([_mx(xs_ref[c]) for c in range(4)], axis=1)
        from_state = jnp.dot(states, mout_ref[gp], preferred_element_type=F32)
        for e in range(2):
            yg_ref[2 * gp + e] = (jnp.dot(q[e], t_ref[2 * gp + e], preferred_element_type=F32)
                                  + from_state[:, e * S5_CW:(e + 1) * S5_CW])

    d = d_ref[...]
    for v in range(2):
        halves = [yg_ref[gi, :, v * S5_LANES:(v + 1) * S5_LANES] for gi in range(n_groups)]
        for a, y in enumerate(_block_transpose8(halves)):
            token_rows = pl.ds(8 * v + a, rows, stride=S5_CHUNK)
            y_ref[token_rows, :] = y + d * u_ref[token_rows, :]


def _s5_branch(u2, mats, seqlen):
    t_mat, m_in, m_out, a_step, d_row = mats
    n = u2.shape[0]
    n_chunks = seqlen // S5_CHUNK
    n_groups = S5_LANES // S5_GROUP
    tile_rows = S5_SEQ_TILE * seqlen
    pairs = S5_SEQ_TILE * n_chunks
    once = pl.Buffered(1)
    return pl.pallas_call(
        functools.partial(_s5_kernel, n_chunks=n_chunks),
        name="s5_scan",
        grid=(S5_WIDTH // S5_LANES, n // tile_rows),
        in_specs=[pl.BlockSpec((tile_rows, S5_LANES), lambda s, b: (b, s)),
                  pl.BlockSpec((n_groups, S5_CW, S5_CW), lambda s, b: (s, 0, 0), pipeline_mode=once),
                  pl.BlockSpec((n_groups // 2, 2 * S5_CW, 4 * S5_SLOT), lambda s, b: (s, 0, 0), pipeline_mode=once),
                  pl.BlockSpec((n_groups // 2, 4 * S5_SLOT, 2 * S5_CW), lambda s, b: (s, 0, 0), pipeline_mode=once),
                  pl.BlockSpec((n_groups // 2, 4, S5_SLOT), lambda s, b: (s, 0, 0)),
                  pl.BlockSpec((1, S5_LANES), lambda s, b: (0, s))],
        out_specs=pl.BlockSpec((tile_rows, S5_LANES), lambda s, b: (b, s), pipeline_mode=once),
        out_shape=jax.ShapeDtypeStruct((n, S5_WIDTH), F32),
        scratch_shapes=[pltpu.VMEM((n_groups, pairs, S5_CW), MXU_DTYPE),
                        pltpu.VMEM((n_groups, pairs, S5_CW), F32),
                        pltpu.VMEM((4, pairs, S5_SLOT), F32),
                        pltpu.VMEM((4, pairs, S5_SLOT), F32)],
        compiler_params=pltpu.CompilerParams(dimension_semantics=("arbitrary", "arbitrary"),
                                             vmem_limit_bytes=VMEM_LIMIT),
    )(u2, t_mat, m_in, m_out, a_step, d_row)


def _rwkv_kernel(z_ref, w0_ref, w2_ref, a0_ref, a2_ref, g2_ref, kk_ref, ka_ref, rk_ref,
                 gng_ref, gnb_ref, o_ref, s_ref, y_ref, pa_ref, pg_ref, *, seqlen):
    n_seq = z_ref.shape[0]
    t = RWKV_CHUNK
    n_chunks = seqlen // t
    w = RWKV_WIDTH

    tri_r = lax.broadcasted_iota(jnp.int32, (t, t), 0)
    tri_c = lax.broadcasted_iota(jnp.int32, (t, t), 1)
    pr = lax.broadcasted_iota(jnp.int32, (PAIR, PAIR), 0)
    pc = lax.broadcasted_iota(jnp.int32, (PAIR, PAIR), 1)
    seg_ones = jnp.where(pr // RWKV_HEAD == pc // RWKV_HEAD, 1.0, 0.0).astype(MXU_DTYPE)

    def head_sums(a, op, dot):
        return jnp.concatenate([dot(a[:, p * PAIR:(p + 1) * PAIR], op) for p in range(N_PAIRS)], axis=1)

    eye = jnp.where(pr == pc, 1.0, 0.0).astype(F32)
    same_head = pr // RWKV_HEAD == pc // RWKV_HEAD
    first_head =lax.broadcasted_iota(jnp.int32, (t, PAIR), 1) < RWKV_HEAD
    cum_ops = (jnp.where(tri_r >= tri_c, 1.0, 0.0).astype(MXU_DTYPE),
               jnp.where(tri_r <= tri_c, 1.0, 0.0).astype(MXU_DTYPE))
    sr = lax.broadcasted_iota(jnp.int32, (t, PAIR), 0)
    scol = lax.broadcasted_iota(jnp.int32, (t, PAIR), 1) % t
    strict = (sr > scol, sr < scol)
    incl = (sr >= scol, sr <= scol)

    k_k = kk_ref[...]
    k_a = ka_ref[...]
    r_k = rk_ref[...]

    s_ref[...] = jnp.zeros_like(s_ref)
    y_ref[...] = jnp.zeros_like(y_ref)
    o_ref[...] = jnp.zeros_like(o_ref)

    def stack(ap):
        return jnp.concatenate([jnp.where(first_head, ap, 0.0), jnp.where(first_head, 0.0, ap)], axis=0)

    names = ("a", "r", "b", "k", "v", "be", "ke")
    stacked = ("b", "k", "v")

    def prep_gen(ci_f, ci_b, valid):
        for q, d, ci in [(q, d, ci) for q in range(n_seq) for d, ci in ((0, ci_f), (1, ci_b))]:
            rows = pl.ds(pl.multiple_of(ci * t, t), t)
            zs = z_ref[q, rows, 0:3 * w + 2 * DECAY_RANK + 2 * ICLR_RANK].astype(F32)
            r = zs[:, 0:w]
            k = zs[:, w:2 * w]
            v = zs[:, 2 * w:3 * w]
            w_low = zs[:, 3 * w:3 * w + 2 * DECAY_RANK]
            a_low = zs[:, 3 * w + 2 * DECAY_RANK:3 * w + 2 * DECAY_RANK + 2 * ICLR_RANK]
            w_log = w0_ref[d:d + 1, :] + _dot(jnp.tanh(w_low), w2_ref[d])
            yield
            w_log = -jnp.logaddexp(-w_log, 0.0) - 0.5
            logw = -jnp.exp(w_log)
            iclr = jax.nn.sigmoid(a0_ref[d:d + 1, :] + _dot(a_low, a2_ref[d]))
            yield
            kk = k * k_k
            nrm = jnp.sqrt(head_sums(kk * kk, seg_ones, _dot))
            kk = kk / jnp.maximum(nrm, NORM_EPS)
            yield
            k_dir = k * (1.0 + (iclr - 1.0) * k_a)
            o_ref[q, rows, :] += head_sums(r * k_dir * r_k, seg_ones, _dot) * (v * valid)
            yield
            cum = _dot_exact_lhs(cum_ops[d], logw)
            last = cum[t - 1:t, :] if d == 0 else cum[0:1, :]
            yield
            e_neg = jnp.exp(-cum)
            b_dir = kk * iclr
            e_end = jnp.exp(last - cum)
            vals = (-kk * jnp.exp(cum - logw), r * jnp.exp(cum), b_dir * e_neg, k_dir * e_neg, v,
                    b_dir * e_end, k_dir * e_end)
            yield
            for j, val in enumerate(vals):
                pa_ref[q, d, j] = val
            pg_ref[q, d, 0:1, :] = jnp.exp(last)
            yield

    chains = [(q, d, p) for q in range(n_seq) for d in range(2) for p in range(N_PAIRS)]

    def chain_gen(i):
        rows = (pl.ds(pl.multiple_of(i * t, t), t), pl.ds(pl.multiple_of((n_chunks - 1 - i) * t, t), t))
        slab = {(q, d, p): {nm: pa_ref[q, d, j, :, p * PAIR:(p + 1) * PAIR] for j, nm in enumerate(names)}
                for q, d, p in chains}
        st = {c: {nm: _mx(stack(slab[c][nm])) for nm in stacked} for c in chains}
        g_end = {(q, d): pg_ref[q, d, 0:1, :] for q in range(n_seq) for d in range(2)}
        ar = {c: _mx(jnp.concatenate([slab[c]["a"], slab[c]["r"]], axis=0)) for c in chains}
        sc = {c: _dot_nt(ar[c], jnp.concatenate([st[c]["b"], st[c]["k"]], axis=0)) for c in chains}
        yield
        l_ab = {c: stack(jnp.where(strict[c[1]], sc[c][0:t, 0:PAIR], 0.0)) for c in chains}
        mix = {c: _mx(jnp.concatenate([jnp.where(strict[c[1]], sc[c][0:t, PAIR:2 * PAIR], 0.0),
                                       jnp.where(incl[c[1]], sc[c][t:2 * t, PAIR:2 * PAIR], 0.0)], axis=0))
               for c in chains}
        m_rb = {c: _mx(jnp.where(incl[c[1]], sc[c][t:2 * t, 0:PAIR], 0.0)) for c in chains}
        s_old = {c: s_ref[c] for c in chains}
        x = {c: _dot(jnp.concatenate([ar[c], mix[c]], axis=1),
                     jnp.concatenate([_mx(s_old[c].T), st[c]["v"]], axis=0)) for c in chains}
        yield
        m = {c: _dot(l_ab[c], l_ab[c]) for c in chains}
        inv = {c: eye + l_ab[c] for c in chains}
        yield
        for _ in range(4):
            mp = {c: _dot(m[c], jnp.concatenate([m[c], inv[c]], axis=1)) for c in chains}
            m = {c: mp[c][:, 0:PAIR] for c in chains}
            inv = {c: inv[c] + mp[c][:, PAIR:2 * PAIR] for c in chains}
            yield
        inv = {c: inv[c] + _dot(m[c], inv[c]) for c in chains}
        yield
        u = {c: _dot(inv[c], stack(x[c][0:t])) for c in chains}
        yield
        o = {c: x[c][t:2 * t] + _dot(m_rb[c], u[c]) for c in chains}
        yield
        for c in chains:
            q, d, p = c
            lanes = slice(p * PAIR, (p + 1) * PAIR)
            y_ref[q, rows[d], lanes] += o[c]
            uv_t = jnp.concatenate([u[c].T, st[c]["v"].astype(F32).T], axis=1)
            be, ke = _mx(slab[c]["be"]), _mx(slab[c]["ke"])
            upd = _dot(uv_t, jnp.concatenate([be, be, ke, ke], axis=0))
            s_ref[c] = s_old[c] * g_end[q, d][:, lanes] + jnp.where(same_head, upd, 0.0)
        yield

    def interleave(main, side, side_per_main):
        for _ in main:
            for _ in range(side_per_main):
                next(side, None)
        for _ in side:
            pass

    prep_stages = 7 * 2 * n_seq
    chain_stages = 11

    def body(i, carry):
        nxt = jnp.minimum(i + 1, n_chunks - 1)
        valid = jnp.where(i + 1 < n_chunks, 1.0, 0.0)
        interleave(chain_gen(i), prep_gen(nxt, n_chunks - 1 - nxt, valid), -(-prep_stages // chain_stages))
        return carry

    for _ in prep_gen(0, n_chunks - 1, 1.0):
        pass
    lax.fori_loop(0, n_chunks, body, 0)

    blk = min(seqlen, 256)
    gn_g = gng_ref[...]
    gn_b = gnb_ref[...]
    seg_mean = (seg_ones.astype(F32) * (1.0 / RWKV_HEAD)).astype(MXU_DTYPE)

    def norm_body(i, carry):
        rows = pl.ds(pl.multiple_of(i * blk, blk), blk)
        for q in range(n_seq):
            y = y_ref[q, rows, :]
            dlt = y - head_sums(y, seg_mean, _dot)
            var = head_sums(dlt * dlt, seg_mean, _dot)
            yn = dlt * lax.rsqrt(var + GN_EPS) * gn_g + gn_b
            gate = _dot(jax.nn.sigmoid(z_ref[q, rows, RWKV_IN_W - GATE_RANK:].astype(F32)), g2_ref[...])
            o_ref[q, rows, :] = (yn + o_ref[q, rows, :]) * gate
        return carry

    lax.fori_loop(0, seqlen // blk, norm_body, 0)


def _rwkv_branch(z, p):
    bsz, seqlen, _ = z.shape
    consts = [p["w0"], p["w2"], p["a0"], p["a2"], p["g2"], p["k_k"], p["k_a"], p["r_k"],
              p["gn_g"], p["gn_b"]]
    ns = RWKV_SEQ_TILE
    return pl.pallas_call(
        functools.partial(_rwkv_kernel, seqlen=seqlen),
        name="rwkv",
        grid=(bsz // ns,),
        in_specs=[pl.BlockSpec((ns, seqlen, RWKV_IN_W), lambda i: (i, 0, 0), pipeline_mode=pl.Buffered(1))]
                 + [_const_spec(c.shape) for c in consts],
        out_specs=pl.BlockSpec((ns, seqlen, RWKV_WIDTH), lambda i: (i, 0, 0)),
        out_shape=jax.ShapeDtypeStruct((bsz, seqlen, RWKV_WIDTH), F32),
        scratch_shapes=[pltpu.VMEM((ns, 2, N_PAIRS, PAIR, PAIR), F32),
                        pltpu.VMEM((ns, seqlen, RWKV_WIDTH), F32),
                        pltpu.VMEM((ns, 2, 7, RWKV_CHUNK, RWKV_WIDTH), F32),
                        pltpu.VMEM((ns, 2, 8, RWKV_WIDTH), F32)],
        compiler_params=pltpu.CompilerParams(dimension_semantics=("arbitrary",),
                                             vmem_limit_bytes=VMEM_LIMIT),
    )(z, *consts)


def _merge_kernel(x_ref, ys_ref, yr_ref, wg_ref, wglu_ref, bglu_ref, wpa_ref, wpb_ref, wout_ref,
                  g_ref, b_ref, h_ref):
    tm = x_ref.shape[0]

    def half_tile(rows):
        x = x_ref[rows, :]
        xb = _mx(x)
        act = _gelu_tanh(ys_ref[rows, :])
        glu = _dot(act, wglu_ref[...])
        yield
        out_a = act * jax.nn.sigmoid(glu + bglu_ref[...])
        gate_a = jnp.dot(xb, wg_ref[:, 0:D_MODEL], preferred_element_type=F32)
        yield
        merged = jax.nn.sigmoid(gate_a) * _dot(out_a, wpa_ref[...])
        yield
        gate_b = jnp.dot(xb, wg_ref[:, D_MODEL:2 * D_MODEL], preferred_element_type=F32)
        yield
        merged = merged + jax.nn.sigmoid(gate_b) * _dot(yr_ref[rows, :], wpb_ref[...])
        yield
        h_ref[rows, :] = _layer_norm(ALPHA * x + _dot(merged, wout_ref[...]), g_ref[...], b_ref[...])
        yield

    halves = [half_tile(pl.ds(0, tm // 2)), half_tile(pl.ds(tm // 2, tm // 2))]
    while halves:
        halves = [h for h in halves if next(h, halves) is not halves]


def _merge(x2, ys, yr, p):
    n = x2.shape[0]
    tm = min(TOKEN_TILE, n)
    consts = [p["w_gates"], p["w_glu"], p["b_glu"], p["w_pa"], p["w_pb"], p["w_out"], p["ln1_g"], p["ln1_b"]]
    return pl.pallas_call(
        _merge_kernel,
        name="merge",
        grid=(n // tm,),
        in_specs=[pl.BlockSpec((tm, D_MODEL), lambda i: (i, 0)),
                  pl.BlockSpec((tm, S5_WIDTH), lambda i: (i, 0)),
                  pl.BlockSpec((tm, RWKV_WIDTH), lambda i: (i, 0))]
                 + [_const_spec(c.shape) for c in consts],
        out_specs=pl.BlockSpec((tm, D_MODEL), lambda i: (i, 0)),
        out_shape=jax.ShapeDtypeStruct((n, D_MODEL), F32),
        compiler_params=pltpu.CompilerParams(dimension_semantics=("arbitrary",),
                                             vmem_limit_bytes=VMEM_LIMIT),
    )(x2, ys, yr, *consts)


def _ffn_kernel(h_ref, wg_ref, wu_ref, wd_ref, g_ref, b_ref, y_ref):
    h = h_ref[...]
    hb = _mx(h)
    acc = ALPHA * h
    for cols in (slice(0, FFN_SPLIT), slice(FFN_SPLIT, D_FF)):
        gate = jnp.dot(hb, wg_ref[:, cols], preferred_element_type=F32)
        up = jnp.dot(hb, wu_ref[:, cols], preferred_element_type=F32)
        acc = acc + _dot(gate * jax.nn.sigmoid(gate) * up, wd_ref[cols, :])
    y_ref[...] = _layer_norm(acc, g_ref[...], b_ref[...])


def _ffn(h2, p):
    n = h2.shape[0]
    tm = min(TOKEN_TILE, n)
    consts = [p["w_gate"], p["w_up"], p["w_down"], p["ln2_g"], p["ln2_b"]]
    return pl.pallas_call(
        _ffn_kernel,
        name="ffn",
        grid=(n // tm,),
        in_specs=[pl.BlockSpec((tm, D_MODEL), lambda i: (i, 0))] + [_const_spec(c.shape) for c in consts],
        out_specs=pl.BlockSpec((tm, D_MODEL), lambda i: (i, 0)),
        out_shape=jax.ShapeDtypeStruct((n, D_MODEL), F32),
        compiler_params=pltpu.CompilerParams(dimension_semantics=("arbitrary",),
                                             vmem_limit_bytes=VMEM_LIMIT),
    )(h2, *consts)


def _pad_rank(w2):
    z = jnp.zeros_like(w2[0])
    return jnp.stack([jnp.concatenate([w2[0], z], axis=0), jnp.concatenate([z, w2[1]], axis=0)])


def _row(v):
    return v.astype(F32).reshape(1, -1)


def _encoder_layer(x, p):
    bsz, seqlen, _ = x.shape
    x2 = x.reshape(bsz * seqlen, D_MODEL)
    u, z = _proj(x2, p["w_u"], p["w_z"], p["mu"], seqlen)
    ys = _s5_branch(u, p["s5_mats"], seqlen)
    yr = _rwkv_branch(z.reshape(bsz, seqlen, RWKV_IN_W), p).reshape(bsz * seqlen, RWKV_WIDTH)
    h = _merge(x2, ys, yr, p)
    return _ffn(h, p).reshape(bsz, seqlen, D_MODEL)


def kernel(x_prompt, x_sample, w_in, s5_lam_re, s5_lam_im, s5_log_dt, s5_b_re, s5_b_im, s5_c_re, s5_c_im,
           s5_d, s5_w_glu, s5_b_glu, rwkv_mu, rwkv_w0, rwkv_w2, rwkv_a0, rwkv_a2, rwkv_g2, rwkv_k_k,
           rwkv_k_a, rwkv_r_k, rwkv_gn_g, rwkv_gn_b, w_pa, w_pb, w_out, ln1_g, ln1_b, w_gate, w_up,
           w_down, ln2_g, ln2_b):
    y_prompt, y_sample = x_prompt, x_sample
    for layer in range(w_in.shape[0]):
        wl = w_in[layer]
        gate_start = S5_WIDTH + RWKV_IN_W
        p = {
            "w_u": _mx(wl[:, :S5_WIDTH]),
            "w_z": _mx(wl[:, S5_WIDTH:gate_start]),
            "w_gates": _mx(wl[:, gate_start:]),
            "s5_mats": _s5_chunk_matrices(s5_lam_re[layer], s5_lam_im[layer], s5_log_dt[layer],
                                          s5_b_re[layer], s5_b_im[layer], s5_c_re[layer], s5_c_im[layer],
                                          s5_d[layer]),
            "w_glu": _mx(s5_w_glu[layer]), "b_glu": _row(s5_b_glu[layer]),
            "mu": _row(rwkv_mu[layer]), "w0": rwkv_w0[layer].astype(F32),
            "w2": _mx(_pad_rank(rwkv_w2[layer])), "a0": rwkv_a0[layer].astype(F32),
            "a2": _mx(_pad_rank(rwkv_a2[layer])), "g2": _mx(rwkv_g2[layer]),
            "k_k": _row(rwkv_k_k[layer]), "k_a": _row(rwkv_k_a[layer]), "r_k": _row(rwkv_r_k[layer]),
            "gn_g": _row(rwkv_gn_g[layer]), "gn_b": _row(rwkv_gn_b[layer]),
            "w_pa": _mx(w_pa[layer]), "w_pb": _mx(w_pb[layer]), "w_out": _mx(w_out[layer]),
            "ln1_g": _row(ln1_g[layer]), "ln1_b": _row(ln1_b[layer]),
            "w_gate": _mx(w_gate[layer]), "w_up": _mx(w_up[layer]), "w_down": _mx(w_down[layer]),
            "ln2_g": _row(ln2_g[layer]), "ln2_b": _row(ln2_b[layer]),
        }
        y_prompt = _encoder_layer(y_prompt, p)
        y_sample = _encoder_layer(y_sample, p)
    return (y_prompt, y_sample)
```

```python
import functools

import jax
import jax.numpy as jnp
from jax import lax
from jax.experimental import pallas as pl
from jax.experimental.pallas import tpu as pltpu

F32 = jnp.float32
MXU_DTYPE = jnp.bfloat16

D_MODEL = 1024
S5_WIDTH = 512
S5_GROUP = 16
S5_GROUPS = S5_WIDTH // S5_GROUP
S5_STATE = 64
RWKV_WIDTH = 512
RWKV_HEAD = 64
DECAY_RANK = 64
ICLR_RANK = 64
GATE_RANK = 128
RWKV_IN_W = 3 * RWKV_WIDTH + 2 * DECAY_RANK + 2 * ICLR_RANK + GATE_RANK
D_FF = 2816
DEPTH = 1
ALPHA = (2.0 * DEPTH) ** 0.25
LN_EPS = 1e-5
GN_EPS = 64e-5
NORM_EPS = 1e-12

S5_CHUNK = 16
S5_CW = S5_CHUNK * S5_GROUP
S5_SLOT = 128
S5_LANES = 128
S5_SEQ_TILE = 8
RWKV_CHUNK = 64
PAIR = 2 * RWKV_HEAD
N_PAIRS = RWKV_WIDTH // PAIR
RWKV_SEQ_TILE = 2
TOKEN_TILE = 512
FFN_SPLIT = 1536
VMEM_LIMIT = 56 * 1024 * 1024


def _mx(a):
    return a.astype(MXU_DTYPE)


def _dot(a, b):
    return jnp.dot(_mx(a), _mx(b), preferred_element_type=F32)


def _dot_nt(a, b):
    return lax.dot_general(_mx(a), _mx(b), (((1,), (1,)), ((), ())), preferred_element_type=F32)


def _split2(x):
    hi = x.astype(MXU_DTYPE)
    return hi, (x - hi.astype(F32)).astype(MXU_DTYPE)


def _dot_exact_lhs(m, x):
    hi, lo = _split2(x)
    return jnp.dot(m, hi, preferred_element_type=F32) + jnp.dot(m, lo, preferred_element_type=F32)


def _layer_norm(v, g, b):
    mu = jnp.mean(v, axis=-1, keepdims=True)
    d = v - mu
    var = jnp.mean(d * d, axis=-1, keepdims=True)
    return d * lax.rsqrt(var + LN_EPS) * g + b


def _gelu_tanh(v):
    return 0.5 * v * (1.0 + jnp.tanh(0.7978845608028654 * (v + 0.044715 * (v * v * v))))


def _const_spec(shape):
    nd = len(shape)
    return pl.BlockSpec(shape, lambda *_: (0,) * nd)


def _proj_kernel(x_ref, xp_ref, xn_ref, wu_ref, wz_ref, mu_ref, u_ref, z_ref, *, seqlen):
    tm = x_ref.shape[0]
    x = x_ref[...]
    u_ref[...] = jnp.dot(_mx(x), wu_ref[...], preferred_element_type=F32)
    x_ext = _mx(jnp.concatenate([xp_ref[...], x, xn_ref[...]], axis=0))
    first = pl.program_id(0) * tm
    row = lax.broadcasted_iota(jnp.int32, (tm, 1), 0)
    has_prev = (row > 0) | (first % seqlen != 0)
    has_next = (row < tm - 1) | ((first + tm) % seqlen != 0)
    z_ext = jnp.dot(x_ext, wz_ref[...], preferred_element_type=F32)
    z = z_ext[8:8 + tm]
    prev = jnp.where(has_prev, pltpu.roll(z_ext, 1, 0)[8:8 + tm], 0.0)
    nxt = jnp.where(has_next, pltpu.roll(z_ext, tm + 15, 0)[8:8 + tm], 0.0)
    z_ref[...] = (z + (0.5 * (prev + nxt) - z) * mu_ref[...]).astype(z_ref.dtype)


def _proj(x2, wu, wz, mu, seqlen):
    n = x2.shape[0]
    tm = min(TOKEN_TILE, seqlen)
    blocks8 = tm // 8
    last8 = n // 8 - 1
    return pl.pallas_call(
        functools.partial(_proj_kernel, seqlen=seqlen),
        name="proj",
        grid=(n // tm,),
        in_specs=[pl.BlockSpec((tm, D_MODEL), lambda i: (i, 0)),
                  pl.BlockSpec((8, D_MODEL), lambda i: (jnp.maximum(i * blocks8 - 1, 0), 0)),
                  pl.BlockSpec((8, D_MODEL), lambda i: (jnp.minimum((i + 1) * blocks8, last8), 0)),
                  _const_spec(wu.shape), _const_spec(wz.shape), _const_spec(mu.shape)],
        out_specs=[pl.BlockSpec((tm, S5_WIDTH), lambda i: (i, 0)),
                   pl.BlockSpec((tm, RWKV_IN_W), lambda i: (i, 0))],
        out_shape=[jax.ShapeDtypeStruct((n, S5_WIDTH), F32),
                   jax.ShapeDtypeStruct((n, RWKV_IN_W), MXU_DTYPE)],
        compiler_params=pltpu.CompilerParams(dimension_semantics=("arbitrary",),
                                             vmem_limit_bytes=VMEM_LIMIT),
    )(x2, x2, x2, wu, wz, mu)


def _s5_chunk_matrices(lam_re, lam_im, log_dt, b_re, b_im, c_re, c_im, d_skip):
    hp = lax.Precision.HIGHEST
    t = S5_CHUNK
    g = S5_GROUPS
    dt = jnp.exp(log_dt.astype(F32))[..., None]
    lre = lam_re.astype(F32)
    lim = lam_im.astype(F32)
    zr = lre * dt
    zi = lim * dt
    n = jnp.arange(t + 1, dtype=F32)[:, None, None, None]
    mag = jnp.exp(n * zr)
    pr = mag * jnp.cos(n * zi)
    pi = mag * jnp.sin(n * zi)
    den = lre * lre + lim * lim
    lbr = pr[1] - 1.0
    lbi = pi[1]
    qr = (lbr * lre + lbi * lim) / den
    qi = (lbi * lre - lbr * lim) / den
    bre = b_re.astype(F32)
    bim = b_im.astype(F32)
    bbr = qr[..., None] * bre - qi[..., None] * bim
    bbi = qr[..., None] * bim + qi[..., None] * bre
    wr = pr[..., None] * bbr - pi[..., None] * bbi
    wi = pr[..., None] * bbi + pi[..., None] * bbr
    cre = c_re.astype(F32)
    cim = c_im.astype(F32)
    kern = (jnp.einsum('dghp,ndgpk->ndghk', cre, wr[:t], precision=hp)
            - jnp.einsum('dghp,ndgpk->ndghk', cim, wi[:t], precision=hp))
    jj = jnp.arange(t)[:, None]
    ii = jnp.arange(t)[None, :]
    lag = jnp.arange(t)
    sel_f = ((ii - jj)[..., None] == lag).astype(F32)
    sel_b = ((jj - ii)[..., None] == lag).astype(F32)
    resp = (jnp.einsum('jin,nghk->jighk', sel_f, kern[:, 0], precision=hp)
            + jnp.einsum('jin,nghk->jighk', sel_b, kern[:, 1], precision=hp))
    t_mat = jnp.transpose(resp, (2, 0, 4, 1, 3)).reshape(g, S5_CW, S5_CW)

    def rows_jh(a):
        a = jnp.transpose(a, (1, 0, 3, 2)).reshape(g, S5_CW, S5_STATE)
        return jnp.pad(a, ((0, 0), (0, 0), (0, S5_SLOT - S5_STATE)))

    m_in = jnp.concatenate([rows_jh(wr[:t, 0][::-1]), rows_jh(wi[:t, 0][::-1]),
                            rows_jh(wr[:t, 1]), rows_jh(wi[:t, 1])], axis=-1)

    def cols_ih(a):
        a = jnp.transpose(a, (1, 3, 0, 2)).reshape(g, S5_STATE, S5_CW)
        return jnp.pad(a, ((0, 0), (0, S5_SLOT - S5_STATE), (0, 0)))

    pf_r, pf_i = pr[1:, 0][:, :, None, :], pi[1:, 0][:, :, None, :]
    pb_r, pb_i = pr[1:, 1][::-1][:, :, None, :], pi[1:, 1][::-1][:, :, None, :]
    m_out = jnp.concatenate([
        cols_ih(cre[0][None] * pf_r - cim[0][None] * pf_i),
        cols_ih(-(cre[0][None] * pf_i + cim[0][None] * pf_r)),
        cols_ih(cre[1][None] * pb_r - cim[1][None] * pb_i),
        cols_ih(-(cre[1][None] * pb_i + cim[1][None] * pb_r))], axis=1)
    a_step = jnp.pad(jnp.stack([pr[t, 0], pi[t, 0], pr[t, 1], pi[t, 1]], axis=1),
                     ((0, 0), (0, 0), (0, S5_SLOT - S5_STATE)))

    def halves(x, slot_axis):
        x = x.reshape((g // 2, 2) + x.shape[1:])
        return x[:, 0], jnp.roll(x[:, 1], S5_STATE, axis=slot_axis)

    in_a, in_b = halves(m_in.reshape(g, S5_CW, 4, S5_SLOT), -1)
    m_in = jnp.concatenate([in_a, in_b], axis=1).reshape(g // 2, 2 * S5_CW, 4 * S5_SLOT)
    out_a, out_b = halves(m_out.reshape(g, 4, S5_SLOT, S5_CW), -2)
    m_out = jnp.concatenate([out_a, out_b], axis=-1).reshape(g // 2, 4 * S5_SLOT, 2 * S5_CW)
    step_a, step_b = halves(a_step, -1)
    return (_mx(t_mat), _mx(m_in), _mx(m_out), step_a + step_b, d_skip.astype(F32).reshape(1, S5_WIDTH))


def _block_transpose8(xs):
    xs = list(xs)
    lane_block = lax.broadcasted_iota(jnp.int32, (1, S5_LANES), 1) // S5_GROUP
    for s in (4, 2, 1):
        keep = (lane_block & s) == 0
        for a in range(8):
            if a & s:
                continue
            lo, hi = xs[a], xs[a | s]
            xs[a] = jnp.where(keep, lo, pltpu.roll(hi, S5_GROUP * s, 1))
            xs[a | s] = jnp.where(keep, pltpu.roll(lo, S5_LANES - S5_GROUP * s, 1), hi)
    return xs


def _s5_kernel(u_ref, t_ref, min_ref, mout_ref, a_ref, d_ref, y_ref, q_ref, yg_ref, sin_ref, xs_ref, *, n_chunks):
    bt = S5_SEQ_TILE
    rows = bt * n_chunks
    n_groups = S5_LANES // S5_GROUP
    slot = S5_SLOT

    for v in range(2):
        pieces = [u_ref[pl.ds(8 * v + a, rows, stride=S5_CHUNK), :] for a in range(8)]
        for gi, q in enumerate(_block_transpose8(pieces)):
            q_ref[gi, :, v * S5_LANES:(v + 1) * S5_LANES] = _mx(q)

    zero = jnp.zeros((bt, slot), F32)
    for gp in range(n_groups // 2):
        q = (q_ref[2 * gp], q_ref[2 * gp + 1])
        s_in = jnp.dot(jnp.concatenate(q, axis=1), min_ref[gp], preferred_element_type=F32)
        for c in range(4):
            sin_ref[c] = s_in[:, c * slot:(c + 1) * slot]
        a = a_ref[gp]
        af_re, af_im, ab_re, ab_im = a[0:1], a[1:2], a[2:3], a[3:4]

        def step(k, carry):
            f_re, f_im, b_re, b_im = carry
            rk = pl.ds(k, bt, stride=n_chunks)
            rr = pl.ds(n_chunks - 1 - k, bt, stride=n_chunks)
            sf_re = sin_ref[0, rk, :]
            sf_im = sin_ref[1, rk, :]
            sb_re = sin_ref[2, rr, :]
            sb_im = sin_ref[3, rr, :]
            xs_ref[0, rk, :] = f_re
            xs_ref[1, rk, :] = f_im
            xs_ref[2, rr, :] = b_re
            xs_ref[3, rr, :] = b_im
            return (af_re * f_re - af_im * f_im + sf_re, af_re * f_im + af_im * f_re + sf_im,
                    ab_re * b_re - ab_im * b_im + sb_re, ab_re * b_im + ab_im * b_re + sb_im)

        lax.fori_loop(0, n_chunks, step, (zero, zero, zero, zero), unroll=4)
        states = jnp.concatenate([_mx(xs_ref[c]) for c in range(4)], axis=1)
        from_state = jnp.dot(states, mout_ref[gp], preferred_element_type=F32)
        for e in range(2):
            yg_ref[2 * gp + e] = (jnp.dot(q[e], t_ref[2 * gp + e], preferred_element_type=F32)
                                  + from_state[:, e * S5_CW:(e + 1) * S5_CW])

    d = d_ref[...]
    for v in range(2):
        halves = [yg_ref[gi, :, v * S5_LANES:(v + 1) * S5_LANES] for gi in range(n_groups)]
        for a, y in enumerate(_block_transpose8(halves)):
            token_rows = pl.ds(8 * v + a, rows, stride=S5_CHUNK)
            y_ref[token_rows, :] = y + d * u_ref[token_rows, :]


def _s5_branch(u2, mats, seqlen):
    t_mat, m_in, m_out, a_step, d_row = mats
    n = u2.shape[0]
    n_chunks = seqlen // S5_CHUNK
    n_groups = S5_LANES // S5_GROUP
    tile_rows = S5_SEQ_TILE * seqlen
    pairs = S5_SEQ_TILE * n_chunks
    once = pl.Buffered(1)
    return pl.pallas_call(
        functools.partial(_s5_kernel, n_chunks=n_chunks),
        name="s5_scan",
        grid=(S5_WIDTH // S5_LANES, n // tile_rows),
        in_specs=[pl.BlockSpec((tile_rows, S5_LANES), lambda s, b: (b, s)),
                  pl.BlockSpec((n_groups, S5_CW, S5_CW), lambda s, b: (s, 0, 0), pipeline_mode=once),
                  pl.BlockSpec((n_groups // 2, 2 * S5_CW, 4 * S5_SLOT), lambda s, b: (s, 0, 0), pipeline_mode=once),
                  pl.BlockSpec((n_groups // 2, 4 * S5_SLOT, 2 * S5_CW), lambda s, b: (s, 0, 0), pipeline_mode=once),
                  pl.BlockSpec((n_groups // 2, 4, S5_SLOT), lambda s, b: (s, 0, 0)),
                  pl.BlockSpec((1, S5_LANES), lambda s, b: (0, s))],
        out_specs=pl.BlockSpec((tile_rows, S5_LANES), lambda s, b: (b, s), pipeline_mode=once),
        out_shape=jax.ShapeDtypeStruct((n, S5_WIDTH), F32),
        scratch_shapes=[pltpu.VMEM((n_groups, pairs, S5_CW), MXU_DTYPE),
                        pltpu.VMEM((n_groups, pairs, S5_CW), F32),
                        pltpu.VMEM((4, pairs, S5_SLOT), F32),
                        pltpu.VMEM((4, pairs, S5_SLOT), F32)],
        compiler_params=pltpu.CompilerParams(dimension_semantics=("arbitrary", "arbitrary"),
                                             vmem_limit_bytes=VMEM_LIMIT),
    )(u2, t_mat, m_in, m_out, a_step, d_row)


def _rwkv_kernel(z_ref, w0_ref, w2_ref, a0_ref, a2_ref, g2_ref, kk_ref, ka_ref, rk_ref,
                 gng_ref, gnb_ref, o_ref, s_ref, y_ref, pa_ref, pg_ref, *, seqlen):
    n_seq = z_ref.shape[0]
    t = RWKV_CHUNK
    n_chunks = seqlen // t
    w = RWKV_WIDTH

    tri_r = lax.broadcasted_iota(jnp.int32, (t, t), 0)
    tri_c = lax.broadcasted_iota(jnp.int32, (t, t), 1)
    pr = lax.broadcasted_iota(jnp.int32, (PAIR, PAIR), 0)
    pc = lax.broadcasted_iota(jnp.int32, (PAIR, PAIR), 1)
    seg_ones = jnp.where(pr // RWKV_HEAD == pc // RWKV_HEAD, 1.0, 0.0).astype(MXU_DTYPE)

    def head_sums(a, op, dot):
        return jnp.concatenate([dot(a[:, p * PAIR:(p + 1) * PAIR], op) for p in range(N_PAIRS)], axis=1)

    eye = jnp.where(pr == pc, 1.0, 0.0).astype(F32)
    same_head = pr // RWKV_HEAD == pc // RWKV_HEAD
    first_head =lax.broadcasted_iota(jnp.int32, (t, PAIR), 1) < RWKV_HEAD
    cum_ops = (jnp.where(tri_r >= tri_c, 1.0, 0.0).astype(MXU_DTYPE),
               jnp.where(tri_r <= tri_c, 1.0, 0.0).astype(MXU_DTYPE))
    sr = lax.broadcasted_iota(jnp.int32, (t, PAIR), 0)
    scol = lax.broadcasted_iota(jnp.int32, (t, PAIR), 1) % t
    strict = (sr > scol, sr < scol)
    incl = (sr >= scol, sr <= scol)

    k_k = kk_ref[...]
    k_a = ka_ref[...]
    r_k = rk_ref[...]

    s_ref[...] = jnp.zeros_like(s_ref)
    y_ref[...] = jnp.zeros_like(y_ref)
    o_ref[...] = jnp.zeros_like(o_ref)

    def stack(ap):
        return jnp.concatenate([jnp.where(first_head, ap, 0.0), jnp.where(first_head, 0.0, ap)], axis=0)

    names = ("a", "r", "b", "k", "v", "be", "ke")
    stacked = ("b", "k", "v")

    def prep_gen(ci_f, ci_b, valid):
        for q, d, ci in [(q, d, ci) for q in range(n_seq) for d, ci in ((0, ci_f), (1, ci_b))]:
            rows = pl.ds(pl.multiple_of(ci * t, t), t)
            zs = z_ref[q, rows, 0:3 * w + 2 * DECAY_RANK + 2 * ICLR_RANK].astype(F32)
            r = zs[:, 0:w]
            k = zs[:, w:2 * w]
            v = zs[:, 2 * w:3 * w]
            w_low = zs[:, 3 * w:3 * w + 2 * DECAY_RANK]
            a_low = zs[:, 3 * w + 2 * DECAY_RANK:3 * w + 2 * DECAY_RANK + 2 * ICLR_RANK]
            w_log = w0_ref[d:d + 1, :] + _dot(jnp.tanh(w_low), w2_ref[d])
            yield
            w_log = -jnp.logaddexp(-w_log, 0.0) - 0.5
            logw = -jnp.exp(w_log)
            iclr = jax.nn.sigmoid(a0_ref[d:d + 1, :] + _dot(a_low, a2_ref[d]))
            yield
            kk = k * k_k
            nrm = jnp.sqrt(head_sums(kk * kk, seg_ones, _dot))
            kk = kk / jnp.maximum(nrm, NORM_EPS)
            yield
            k_dir = k * (1.0 + (iclr - 1.0) * k_a)
            o_ref[q, rows, :] += head_sums(r * k_dir * r_k, seg_ones, _dot) * (v * valid)
            yield
            cum = _dot_exact_lhs(cum_ops[d], logw)
            last = cum[t - 1:t, :] if d == 0 else cum[0:1, :]
            yield
            e_neg = jnp.exp(-cum)
            b_dir = kk * iclr
            e_end = jnp.exp(last - cum)
            vals = (-kk * jnp.exp(cum - logw), r * jnp.exp(cum), b_dir * e_neg, k_dir * e_neg, v,
                    b_dir * e_end, k_dir * e_end)
            yield
            for j, val in enumerate(vals):
                pa_ref[q, d, j] = val
            pg_ref[q, d, 0:1, :] = jnp.exp(last)
            yield

    def chain_gen(i, q):
        chains = [(q, d, p) for d in range(2) for p in range(N_PAIRS)]
        rows = (pl.ds(pl.multiple_of(i * t, t), t), pl.ds(pl.multiple_of((n_chunks - 1 - i) * t, t), t))
        slab = {(q, d, p): {nm: pa_ref[q, d, j, :, p * PAIR:(p + 1) * PAIR] for j, nm in enumerate(names)}
                for q, d, p in chains}
        st = {c: {nm: _mx(stack(slab[c][nm])) for nm in stacked} for c in chains}
        g_end = {(q, d): pg_ref[q, d, 0:1, :] for d in range(2)}
        ar = {c: _mx(jnp.concatenate([slab[c]["a"], slab[c]["r"]], axis=0)) for c in chains}
        sc = {c: _dot_nt(ar[c], jnp.concatenate([st[c]["b"], st[c]["k"]], axis=0)) for c in chains}
        yield
        l_ab = {c: stack(jnp.where(strict[c[1]], sc[c][0:t, 0:PAIR], 0.0)) for c in chains}
        mix = {c: _mx(jnp.concatenate([jnp.where(strict[c[1]], sc[c][0:t, PAIR:2 * PAIR], 0.0),
                                       jnp.where(incl[c[1]], sc[c][t:2 * t, PAIR:2 * PAIR], 0.0)], axis=0))
               for c in chains}
        m_rb = {c: _mx(jnp.where(incl[c[1]], sc[c][t:2 * t, 0:PAIR], 0.0)) for c in chains}
        s_old = {c: s_ref[c] for c in chains}
        x = {c: _dot(jnp.concatenate([ar[c], mix[c]], axis=1),
                     jnp.concatenate([_mx(s_old[c].T), st[c]["v"]], axis=0)) for c in chains}
        yield
        m = {c: _dot(l_ab[c], l_ab[c]) for c in chains}
        inv = {c: eye + l_ab[c] for c in chains}
        yield
        for _ in range(4):
            mp = {c: _dot(m[c], jnp.concatenate([m[c], inv[c]], axis=1)) for c in chains}
            m = {c: mp[c][:, 0:PAIR] for c in chains}
            inv = {c: inv[c] + mp[c][:, PAIR:2 * PAIR] for c in chains}
            yield
        inv = {c: inv[c] + _dot(m[c], inv[c]) for c in chains}
        yield
        u = {c: _dot(inv[c], stack(x[c][0:t])) for c in chains}
        yield
        o = {c: x[c][t:2 * t] + _dot(m_rb[c], u[c]) for c in chains}
        yield
        for c in chains:
            q, d, p = c
            lanes = slice(p * PAIR, (p + 1) * PAIR)
            y_ref[q, rows[d], lanes] += o[c]
            uv_t = jnp.concatenate([u[c].T, st[c]["v"].astype(F32).T], axis=1)
            be, ke = _mx(slab[c]["be"]), _mx(slab[c]["ke"])
            upd = _dot(uv_t, jnp.concatenate([be, be, ke, ke], axis=0))
            s_ref[c] = s_old[c] * g_end[q, d][:, lanes] + jnp.where(same_head, upd, 0.0)
        yield

    def interleave(mains, side, side_per_stage):
        mains = list(mains)
        while mains:
            for g in list(mains):
                if next(g, mains) is mains:
                    mains.remove(g)
                for _ in range(side_per_stage):
                    next(side, None)
        for _ in side:
            pass

    prep_stages = 7 * 2 * n_seq
    chain_stages = 11 * n_seq
    side_per_stage = -(-prep_stages // chain_stages)
    assert side_per_stage * n_seq < 7

    def body(i, carry):
        nxt = jnp.minimum(i + 1, n_chunks - 1)
        valid = jnp.where(i + 1 < n_chunks, 1.0, 0.0)
        interleave([chain_gen(i, q) for q in range(n_seq)], prep_gen(nxt, n_chunks - 1 - nxt, valid),
                   side_per_stage)
        return carry

    for _ in prep_gen(0, n_chunks - 1, 1.0):
        pass
    lax.fori_loop(0, n_chunks, body, 0)

    blk = min(seqlen, 256)
    gn_g = gng_ref[...]
    gn_b = gnb_ref[...]
    seg_mean = (seg_ones.astype(F32) * (1.0 / RWKV_HEAD)).astype(MXU_DTYPE)

    def norm_body(i, carry):
        rows = pl.ds(pl.multiple_of(i * blk, blk), blk)
        for q in range(n_seq):
            y = y_ref[q, rows, :]
            dlt = y - head_sums(y, seg_mean, _dot)
            var = head_sums(dlt * dlt, seg_mean, _dot)
            yn = dlt * lax.rsqrt(var + GN_EPS) * gn_g + gn_b
            gate = _dot(jax.nn.sigmoid(z_ref[q, rows, RWKV_IN_W - GATE_RANK:].astype(F32)), g2_ref[...])
            o_ref[q, rows, :] = (yn + o_ref[q, rows, :]) * gate
        return carry

    lax.fori_loop(0, seqlen // blk, norm_body, 0)


def _rwkv_branch(z, p):
    bsz, seqlen, _ = z.shape
    consts = [p["w0"], p["w2"], p["a0"], p["a2"], p["g2"], p["k_k"], p["k_a"], p["r_k"],
              p["gn_g"], p["gn_b"]]
    ns = RWKV_SEQ_TILE
    return pl.pallas_call(
        functools.partial(_rwkv_kernel, seqlen=seqlen),
        name="rwkv",
        grid=(bsz // ns,),
        in_specs=[pl.BlockSpec((ns, seqlen, RWKV_IN_W), lambda i: (i, 0, 0), pipeline_mode=pl.Buffered(1))]
                 + [_const_spec(c.shape) for c in consts],
        out_specs=pl.BlockSpec((ns, seqlen, RWKV_WIDTH), lambda i: (i, 0, 0)),
        out_shape=jax.ShapeDtypeStruct((bsz, seqlen, RWKV_WIDTH), F32),
        scratch_shapes=[pltpu.VMEM((ns, 2, N_PAIRS, PAIR, PAIR), F32),
                        pltpu.VMEM((ns, seqlen, RWKV_WIDTH), F32),
                        pltpu.VMEM((ns, 2, 7, RWKV_CHUNK, RWKV_WIDTH), F32),
                        pltpu.VMEM((ns, 2, 8, RWKV_WIDTH), F32)],
        compiler_params=pltpu.CompilerParams(dimension_semantics=("arbitrary",),
                                             vmem_limit_bytes=VMEM_LIMIT),
    )(z, *consts)


def _merge_kernel(x_ref, ys_ref, yr_ref, wg_ref, wglu_ref, bglu_ref, wpa_ref, wpb_ref, wout_ref,
                  g_ref, b_ref, h_ref):
    tm = x_ref.shape[0]

    def half_tile(rows):
        x = x_ref[rows, :]
        xb = _mx(x)
        act = _gelu_tanh(ys_ref[rows, :])
        glu = _dot(act, wglu_ref[...])
        yield
        out_a = act * jax.nn.sigmoid(glu + bglu_ref[...])
        gate_a = jnp.dot(xb, wg_ref[:, 0:D_MODEL], preferred_element_type=F32)
        yield
        merged = jax.nn.sigmoid(gate_a) * _dot(out_a, wpa_ref[...])
        yield
        gate_b = jnp.dot(xb, wg_ref[:, D_MODEL:2 * D_MODEL], preferred_element_type=F32)
        yield
        merged = merged + jax.nn.sigmoid(gate_b) * _dot(yr_ref[rows, :], wpb_ref[...])
        yield
        h_ref[rows, :] = _layer_norm(ALPHA * x + _dot(merged, wout_ref[...]), g_ref[...], b_ref[...])
        yield

    halves = [half_tile(pl.ds(0, tm // 2)), half_tile(pl.ds(tm // 2, tm // 2))]
    while halves:
        halves = [h for h in halves if next(h, halves) is not halves]


def _merge(x2, ys, yr, p):
    n = x2.shape[0]
    tm = min(TOKEN_TILE, n)
    consts = [p["w_gates"], p["w_glu"], p["b_glu"], p["w_pa"], p["w_pb"], p["w_out"], p["ln1_g"], p["ln1_b"]]
    return pl.pallas_call(
        _merge_kernel,
        name="merge",
        grid=(n // tm,),
        in_specs=[pl.BlockSpec((tm, D_MODEL), lambda i: (i, 0)),
                  pl.BlockSpec((tm, S5_WIDTH), lambda i: (i, 0)),
                  pl.BlockSpec((tm, RWKV_WIDTH), lambda i: (i, 0))]
                 + [_const_spec(c.shape) for c in consts],
        out_specs=pl.BlockSpec((tm, D_MODEL), lambda i: (i, 0)),
        out_shape=jax.ShapeDtypeStruct((n, D_MODEL), F32),
        compiler_params=pltpu.CompilerParams(dimension_semantics=("arbitrary",),
                                             vmem_limit_bytes=VMEM_LIMIT),
    )(x2, ys, yr, *consts)


def _ffn_kernel(h_ref, wg_ref, wu_ref, wd_ref, g_ref, b_ref, y_ref):
    tm = h_ref.shape[0]

    def half_tile(rows):
        h = h_ref[rows, :]
        hb = _mx(h)
        acc = ALPHA * h
        for cols in (slice(0, FFN_SPLIT), slice(FFN_SPLIT, D_FF)):
            gate = jnp.dot(hb, wg_ref[:, cols], preferred_element_type=F32)
            up = jnp.dot(hb, wu_ref[:, cols], preferred_element_type=F32)
            yield
            acc = acc + _dot(gate * jax.nn.sigmoid(gate) * up, wd_ref[cols, :])
            yield
        y_ref[rows, :] = _layer_norm(acc, g_ref[...], b_ref[...])
        yield

    halves = [half_tile(pl.ds(0, tm // 2)), half_tile(pl.ds(tm // 2, tm // 2))]
    while halves:
        halves = [h for h in halves if next(h, halves) is not halves]


def _ffn(h2, p):
    n = h2.shape[0]
    tm = min(TOKEN_TILE, n)
    consts = [p["w_gate"], p["w_up"], p["w_down"], p["ln2_g"], p["ln2_b"]]
    return pl.pallas_call(
        _ffn_kernel,
        name="ffn",
        grid=(n // tm,),
        in_specs=[pl.BlockSpec((tm, D_MODEL), lambda i: (i, 0))] + [_const_spec(c.shape) for c in consts],
        out_specs=pl.BlockSpec((tm, D_MODEL), lambda i: (i, 0)),
        out_shape=jax.ShapeDtypeStruct((n, D_MODEL), F32),
        compiler_params=pltpu.CompilerParams(dimension_semantics=("arbitrary",),
                                             vmem_limit_bytes=VMEM_LIMIT),
    )(h2, *consts)


def _pad_rank(w2):
    z = jnp.zeros_like(w2[0])
    return jnp.stack([jnp.concatenate([w2[0], z], axis=0), jnp.concatenate([z, w2[1]], axis=0)])


def _row(v):
    return v.astype(F32).reshape(1, -1)


def _encoder_layer(x, p):
    bsz, seqlen, _ = x.shape
    x2 = x.reshape(bsz * seqlen, D_MODEL)
    u, z = _proj(x2, p["w_u"], p["w_z"], p["mu"], seqlen)
    ys = _s5_branch(u, p["s5_mats"], seqlen)
    yr = _rwkv_branch(z.reshape(bsz, seqlen, RWKV_IN_W), p).reshape(bsz * seqlen, RWKV_WIDTH)
    h = _merge(x2, ys, yr, p)
    return _ffn(h, p).reshape(bsz, seqlen, D_MODEL)


def kernel(x_prompt, x_sample, w_in, s5_lam_re, s5_lam_im, s5_log_dt, s5_b_re, s5_b_im, s5_c_re, s5_c_im,
           s5_d, s5_w_glu, s5_b_glu, rwkv_mu, rwkv_w0, rwkv_w2, rwkv_a0, rwkv_a2, rwkv_g2, rwkv_k_k,
           rwkv_k_a, rwkv_r_k, rwkv_gn_g, rwkv_gn_b, w_pa, w_pb, w_out, ln1_g, ln1_b, w_gate, w_up,
           w_down, ln2_g, ln2_b):
    y_prompt, y_sample = x_prompt, x_sample
    for layer in range(w_in.shape[0]):
        wl = w_in[layer]
        gate_start = S5_WIDTH + RWKV_IN_W
        p = {
            "w_u": _mx(wl[:, :S5_WIDTH]),
            "w_z": _mx(wl[:, S5_WIDTH:gate_start]),
            "w_gates": _mx(wl[:, gate_start:]),
            "s5_mats": _s5_chunk_matrices(s5_lam_re[layer], s5_lam_im[layer], s5_log_dt[layer],
                                          s5_b_re[layer], s5_b_im[layer], s5_c_re[layer], s5_c_im[layer],
                                          s5_d[layer]),
            "w_glu": _mx(s5_w_glu[layer]), "b_glu": _row(s5_b_glu[layer]),
            "mu": _row(rwkv_mu[layer]), "w0": rwkv_w0[layer].astype(F32),
            "w2": _mx(_pad_rank(rwkv_w2[layer])), "a0": rwkv_a0[layer].astype(F32),
            "a2": _mx(_pad_rank(rwkv_a2[layer])), "g2": _mx(rwkv_g2[layer]),
            "k_k": _row(rwkv_k_k[layer]), "k_a": _row(rwkv_k_a[layer]), "r_k": _row(rwkv_r_k[layer]),
            "gn_g": _row(rwkv_gn_g[layer]), "gn_b": _row(rwkv_gn_b[layer]),
            "w_pa": _mx(w_pa[layer]), "w_pb": _mx(w_pb[layer]), "w_out": _mx(w_out[layer]),
            "ln1_g": _row(ln1_g[layer]), "ln1_b": _row(ln1_b[layer]),
            "w_gate": _mx(w_gate[layer]), "w_up": _mx(w_up[layer]), "w_down": _mx(w_down[layer]),
            "ln2_g": _row(ln2_g[layer]), "ln2_b": _row(ln2_b[layer]),
        }
        y_prompt = _encoder_layer(y_prompt, p)
        y_sample = _encoder_layer(y_sample, p)
    return (y_prompt, y_sample)
```
